```python
import jax, jax.numpy as jnp
from jax import lax
import numpy as np

D_MODEL = 1024
BATCH = 2
SEQ = 8192
DEPTH = 1

N_META = 16
D_MIX = D_MODEL
LRU_WIDTH = D_MIX // 2
LRU_HEADS = 8
LRU_HEAD_DIM = LRU_WIDTH // LRU_HEADS
RG_LRU_C = 8.0
CONV_WIDTH = 4
CONV_LEFT = 2
FOURIER_WIDTH = D_MIX - LRU_WIDTH
FOURIER_GROUPS = 8
FOURIER_GROUP_DIM = FOURIER_WIDTH // FOURIER_GROUPS
D_IN_PROJ = 2 * LRU_WIDTH + FOURIER_WIDTH
D_FF = 2816
EPS = 1e-6

kernel_name = "hybrid_rglru_fnet_macaron_encoder"


def rms_norm(x, g):
    xf = x.astype(jnp.float32)
    y = xf * lax.rsqrt(jnp.mean(xf * xf, axis=-1, keepdims=True) + EPS)
    return (y * g.astype(jnp.float32)).astype(x.dtype)


def swiglu(h, w_in, w_out):
    gate, up = jnp.split(h @ w_in, 2, axis=-1)
    return (jax.nn.silu(gate) * up) @ w_out


def centred_depthwise_conv(x, w, b):
    T = x.shape[1]
    xp = jnp.pad(x, ((0, 0), (CONV_LEFT, CONV_WIDTH - 1 - CONV_LEFT), (0, 0)))
    out = xp[:, 0:T] * w[0]
    for k in range(1, CONV_WIDTH):
        out = out + xp[:, k:k + T] * w[k]
    return out + b


def linear_recurrence(a, b):
    def combine(l, r):
        return (l[0] * r[0], r[0] * l[1] + r[1])
    _, h = lax.associative_scan(combine, (a, b), axis=1)
    return h


def block_diag(x, w, b):
    B, T, _ = x.shape
    H, Dh, _ = w.shape
    y = jnp.einsum('bthi,hij->bthj', x.reshape(B, T, H, Dh), w)
    return y.reshape(B, T, H * Dh) + b


def rg_lru(xc, wa, ba, wx, bx, lam):
    r = jax.nn.sigmoid(block_diag(xc, wa, ba))
    i = jax.nn.sigmoid(block_diag(xc, wx, bx))
    log_a = -RG_LRU_C * r * jax.nn.softplus(-lam)
    a = jnp.exp(log_a)
    mult = jnp.sqrt(-jnp.expm1(2.0 * log_a))
    return linear_recurrence(a, mult * (i * xc))


def fourier_mix(v, w, b):
    B, T, _ = v.shape
    vg = v.astype(jnp.float32).reshape(B, T, FOURIER_GROUPS, FOURIER_GROUP_DIM)
    f = jnp.fft.fft2(vg, axes=(1, 3), norm='ortho').real
    y = jnp.einsum('btgi,gij->btgj', f, w.astype(jnp.float32))
    return y.reshape(B, T, FOURIER_WIDTH) + b.astype(jnp.float32)


def setup_inputs(seed: int = 0) -> dict:
    key = jax.random.key(seed)
    ks = jax.random.split(key, 32)
    nrm = lambda k, shape, scale: jax.random.normal(k, shape, jnp.float32) * scale
    gain = lambda k, shape: 1.0 + 0.02 * jax.random.normal(k, shape, jnp.float32)

    def lru_lambda(k):
        ac = jax.random.uniform(k, (DEPTH, LRU_WIDTH), jnp.float32, 0.9, 0.999)
        a = ac ** (1.0 / RG_LRU_C)
        return jnp.log(a) - jnp.log1p(-a)

    hd = LRU_HEAD_DIM ** -0.5
    return {
        "x": nrm(ks[0], (BATCH, SEQ, D_MODEL), 1.0),
        "meta_tokens": nrm(ks[1], (N_META, D_MODEL), 1.0),
        "norm_ffn1": gain(ks[2], (DEPTH, D_MODEL)),
        "w_ffn1_in": nrm(ks[3], (DEPTH, D_MODEL, 2 * D_FF), D_MODEL ** -0.5),
        "w_ffn1_out": nrm(ks[4], (DEPTH, D_FF, D_MODEL), D_FF ** -0.5),
        "norm_mix": gain(ks[5], (DEPTH, D_MODEL)),
        "w_in": nrm(ks[6], (DEPTH, D_MODEL, D_IN_PROJ), D_MODEL ** -0.5),
        "conv_w": nrm(ks[7], (DEPTH, CONV_WIDTH, LRU_WIDTH), CONV_WIDTH ** -0.5),
        "conv_b": nrm(ks[8], (DEPTH, LRU_WIDTH), 0.02),
        "lru_wa_fwd": nrm(ks[9], (DEPTH, LRU_HEADS, LRU_HEAD_DIM, LRU_HEAD_DIM), hd),
        "lru_ba_fwd": nrm(ks[10], (DEPTH, LRU_WIDTH), 0.02),
        "lru_wx_fwd": nrm(ks[11], (DEPTH, LRU_HEADS, LRU_HEAD_DIM, LRU_HEAD_DIM), hd),
        "lru_bx_fwd": nrm(ks[12], (DEPTH, LRU_WIDTH), 0.02),
        "lru_lambda_fwd": lru_lambda(ks[13]),
        "lru_wa_bwd": nrm(ks[14], (DEPTH, LRU_HEADS, LRU_HEAD_DIM, LRU_HEAD_DIM), hd),
        "lru_ba_bwd": nrm(ks[15], (DEPTH, LRU_WIDTH), 0.02),
        "lru_wx_bwd": nrm(ks[16], (DEPTH, LRU_HEADS, LRU_HEAD_DIM, LRU_HEAD_DIM), hd),
        "lru_bx_bwd": nrm(ks[17], (DEPTH, LRU_WIDTH), 0.02),
        "lru_lambda_bwd": lru_lambda(ks[18]),
        "fourier_w": nrm(ks[19], (DEPTH, FOURIER_GROUPS, FOURIER_GROUP_DIM, FOURIER_GROUP_DIM), FOURIER_GROUP_DIM ** -0.5),
        "fourier_b": nrm(ks[20], (DEPTH, FOURIER_WIDTH), 0.02),
        "norm_lru_out": gain(ks[21], (DEPTH, LRU_WIDTH)),
        "norm_fourier_out": gain(ks[22], (DEPTH, FOURIER_WIDTH)),
        "w_out": nrm(ks[23], (DEPTH, D_MIX, D_MODEL), D_MIX ** -0.5),
        "norm_ffn2": gain(ks[24], (DEPTH, D_MODEL)),
        "w_ffn2_in": nrm(ks[25], (DEPTH, D_MODEL, 2 * D_FF), D_MODEL ** -0.5),
        "w_ffn2_out": nrm(ks[26], (DEPTH, D_FF, D_MODEL), D_FF ** -0.5),
        "norm_final": gain(ks[27], (D_MODEL,)),
    }


def reference(x, meta_tokens, norm_ffn1, w_ffn1_in, w_ffn1_out, norm_mix, w_in, conv_w, conv_b,
              lru_wa_fwd, lru_ba_fwd, lru_wx_fwd, lru_bx_fwd, lru_lambda_fwd,
              lru_wa_bwd, lru_ba_bwd, lru_wx_bwd, lru_bx_bwd, lru_lambda_bwd,
              fourier_w, fourier_b, norm_lru_out, norm_fourier_out, w_out,
              norm_ffn2, w_ffn2_in, w_ffn2_out, norm_final):
    B = x.shape[0]
    meta = jnp.broadcast_to(meta_tokens.astype(x.dtype)[None], (B, N_META, D_MODEL))
    h = jnp.concatenate([meta, x], axis=1)

    for l in range(DEPTH):
        h = h + 0.5 * swiglu(rms_norm(h, norm_ffn1[l]), w_ffn1_in[l], w_ffn1_out[l])

        u = rms_norm(h, norm_mix[l]) @ w_in[l]
        lru_x = u[..., :LRU_WIDTH]
        lru_gate = u[..., LRU_WIDTH:2 * LRU_WIDTH]
        four_v = u[..., 2 * LRU_WIDTH:]

        xc = centred_depthwise_conv(lru_x, conv_w[l], conv_b[l]).astype(jnp.float32)
        h_fwd = rg_lru(xc, lru_wa_fwd[l].astype(jnp.float32), lru_ba_fwd[l].astype(jnp.float32),
                       lru_wx_fwd[l].astype(jnp.float32), lru_bx_fwd[l].astype(jnp.float32),
                       lru_lambda_fwd[l].astype(jnp.float32))
        h_bwd = jnp.flip(rg_lru(jnp.flip(xc, axis=1), lru_wa_bwd[l].astype(jnp.float32),
                                lru_ba_bwd[l].astype(jnp.float32), lru_wx_bwd[l].astype(jnp.float32),
                                lru_bx_bwd[l].astype(jnp.float32), lru_lambda_bwd[l].astype(jnp.float32)), axis=1)
        y_lru = (h_fwd + h_bwd) * jax.nn.gelu(lru_gate.astype(jnp.float32))
        y_lru = rms_norm(y_lru, norm_lru_out[l]).astype(h.dtype)

        y_four = rms_norm(fourier_mix(four_v, fourier_w[l], fourier_b[l]), norm_fourier_out[l]).astype(h.dtype)

        h = h + jnp.concatenate([y_lru, y_four], axis=-1) @ w_out[l]

        h = h + 0.5 * swiglu(rms_norm(h, norm_ffn2[l]), w_ffn2_in[l], w_ffn2_out[l])

    return rms_norm(h, norm_final)[:, N_META:]
```

```python
import functools
import math

import numpy as np
import jax
import jax.numpy as jnp
from jax import lax
from jax.experimental import pallas as pl
from jax.experimental.pallas import tpu as pltpu

D_MODEL = 1024
N_META = 16
LRU_WIDTH = 512
LRU_HEADS = 8
LRU_HEAD_DIM = 64
FOURIER_WIDTH = 512
FOURIER_GROUPS = 8
FOURIER_GROUP_DIM = 64
D_FF = 2816
EPS = 1e-6
RG_LRU_C = 8.0

LANES = 128
SUBLANES = 8
ROW_TILE = 512
FF_CHUNK = 1408
VMEM_LIMIT = 60 * 1024 * 1024

N2 = N_META
N1 = 513
N1_PAD = 520
N_SEG = 16
SEG_LEN = 513
LRU_CHUNK = 456

f32 = jnp.float32
bf16 = jnp.bfloat16


def _rms_norm(x, g):
    return x * lax.rsqrt(jnp.mean(x * x, axis=-1, keepdims=True) + EPS) * g


def _sigmoid(x):
    return 0.5 * (1.0 + jnp.tanh(0.5 * x))


def _gelu_tanh(x):
    c = math.sqrt(2.0 / math.pi)
    return 0.5 * x * (1.0 + jnp.tanh(c * (x + 0.044715 * (x * x * x))))


def _swiglu(xn, w1_ref, w2_ref, act_ref):
    for c in range(D_FF // FF_CHUNK):
        lo = c * FF_CHUNK
        g = jnp.dot(xn, w1_ref[:, lo:lo + FF_CHUNK], preferred_element_type=f32)
        u = jnp.dot(xn, w1_ref[:, D_FF + lo:D_FF + lo + FF_CHUNK], preferred_element_type=f32)
        act_ref[:, lo:lo + FF_CHUNK] = (g * _sigmoid(g) * u).astype(bf16)
    return jnp.dot(act_ref[...], w2_ref[...], preferred_element_type=f32)


def _resident(shape):
    nd = len(shape)
    return pl.BlockSpec(shape, lambda *_: (0,) * nd, pipeline_mode=pl.Buffered(1))


def _fold_kernel(w4_ref, bd_ref, cd_ref, sd_ref, o_ref):
    hi = lax.Precision.HIGHEST
    mc = jnp.dot(cd_ref[...], bd_ref[...], preferred_element_type=f32, precision=hi)
    ms = jnp.dot(sd_ref[...], bd_ref[...], preferred_element_type=f32, precision=hi)
    w4 = w4_ref[...]
    o_ref[:, 0:FOURIER_WIDTH] = jnp.dot(w4, mc, preferred_element_type=f32, precision=hi)
    o_ref[:, FOURIER_WIDTH:] = jnp.dot(w4, ms, preferred_element_type=f32, precision=hi)


def _fold_fourier_weights(w_four_in, fourier_w):
    g, n = FOURIER_GROUPS, FOURIER_GROUP_DIM
    eye = np.eye(g, dtype=np.float64)
    ang = 2.0 * np.pi * (np.outer(np.arange(n), np.arange(n)) % n) / n
    scale = 1.0 / math.sqrt((N1 * N2) * n)
    cd = jnp.asarray(np.kron(eye, np.cos(ang)) * scale, f32)
    sd = jnp.asarray(np.kron(eye, np.sin(ang)) * scale, f32)
    bd = (jnp.eye(g, dtype=f32)[:, None, :, None] * fourier_w.astype(f32)[:, :, None, :]).reshape(g * n, g * n)
    return pl.pallas_call(
        _fold_kernel,
        out_shape=jax.ShapeDtypeStruct((D_MODEL, 2 * FOURIER_WIDTH), f32),
        name="fold_fourier",
    )(w_four_in.astype(f32), bd, cd, sd)


def _ffn_in_kernel(x_ref, g1_ref, w1_ref, w2_ref, g2_ref, win_ref,
                   h1_ref, lru_ref, gate_ref, ab_ref, act_ref):
    x = x_ref[...]
    xn = _rms_norm(x, g1_ref[...]).astype(bf16)
    h1 = x + 0.5 * _swiglu(xn, w1_ref, w2_ref, act_ref)
    h1_ref[...] = h1
    xn2 = _rms_norm(h1, g2_ref[...]).astype(bf16)
    u = jnp.dot(xn2, win_ref[...], preferred_element_type=f32)
    lru_ref[...] = u[:, 0:LRU_WIDTH]
    gate_ref[...] = u[:, LRU_WIDTH:2 * LRU_WIDTH]
    ab_ref[...] = u[:, 2 * LRU_WIDTH:].astype(bf16)


def _ffn_in(x2d, g1, w1, w2, g2, win, tile):
    rows = x2d.shape[0]
    n_in = win.shape[1]
    row_spec = lambda w: pl.BlockSpec((tile, w), lambda i: (i, 0))
    return pl.pallas_call(
        _ffn_in_kernel,
        grid=(rows // tile,),
        in_specs=[row_spec(D_MODEL), _resident((1, D_MODEL)), _resident(w1.shape), _resident(w2.shape),
                  _resident((1, D_MODEL)), _resident(win.shape)],
        out_specs=[row_spec(D_MODEL), row_spec(LRU_WIDTH), row_spec(LRU_WIDTH), row_spec(n_in - 2 * LRU_WIDTH)],
        out_shape=[jax.ShapeDtypeStruct((rows, D_MODEL), f32),
                   jax.ShapeDtypeStruct((rows, LRU_WIDTH), f32),
                   jax.ShapeDtypeStruct((rows, LRU_WIDTH), f32),
                   jax.ShapeDtypeStruct((rows, n_in - 2 * LRU_WIDTH), bf16)],
        scratch_shapes=[pltpu.VMEM((tile, D_FF), bf16)],
        compiler_params=pltpu.CompilerParams(dimension_semantics=("arbitrary",), vmem_limit_bytes=VMEM_LIMIT),
        name="ffn_in",
    )(x2d, g1, w1, w2, g2, win)


def _lru_kernel(xl_ref, xm_ref, cw_ref, cb_ref, wg_ref, bg_ref, lam_ref, o_ref,
                seq_ref, af_ref, bf_ref, ab_ref, bb_ref, ends_ref, carry_ref):
    t_all = N_SEG * SEG_LEN
    zero8 = jnp.zeros((SUBLANES, LANES), f32)
    seq_ref[0:SUBLANES, :] = zero8
    seq_ref[SUBLANES:SUBLANES + N_META, :] = xm_ref[...]
    seq_ref[SUBLANES + N_META:SUBLANES + t_all, :] = xl_ref[...]
    seq_ref[SUBLANES + t_all:2 * SUBLANES + t_all, :] = zero8

    cw = cw_ref[...]
    cb = cb_ref[...]
    bg = bg_ref[...]
    lam = lam_ref[...]
    nlam = -lam
    sp = jnp.maximum(nlam, 0.0) + jnp.log1p(jnp.exp(-jnp.abs(nlam)))
    sp_f = sp[0:1, :]
    sp_b = sp[1:2, :]

    def gates(pre_r, pre_i, xc, sp_dir):
        r = _sigmoid(pre_r)
        i = _sigmoid(pre_i)
        t = jnp.tanh((-0.5 * RG_LRU_C) * r * sp_dir)
        q = 1.0 / (1.0 - t)
        a = (1.0 + t) * q
        mult = 2.0 * jnp.sqrt(-t) * q
        return a, mult * (i * xc)

    def chunk_body(c, carry):
        r0 = pl.multiple_of(c * LRU_CHUNK, SUBLANES)
        win = seq_ref[pl.ds(r0, LRU_CHUNK + 2 * SUBLANES), :]
        xc = (cw[0:1, :] * win[6:6 + LRU_CHUNK] + cw[1:2, :] * win[7:7 + LRU_CHUNK]
              + cw[2:3, :] * win[8:8 + LRU_CHUNK] + cw[3:4, :] * win[9:9 + LRU_CHUNK] + cb)
        pre = jnp.dot(xc.astype(bf16), wg_ref[...], preferred_element_type=f32) + bg
        a_f, b_f = gates(pre[:, 0:LANES], pre[:, LANES:2 * LANES], xc, sp_f)
        a_b, b_b = gates(pre[:, 2 * LANES:3 * LANES], pre[:, 3 * LANES:4 * LANES], xc, sp_b)
        rows = pl.ds(r0, LRU_CHUNK)
        af_ref[rows, :] = a_f
        bf_ref[rows, :] = b_f
        ab_ref[rows, :] = a_b
        bb_ref[rows, :] = b_b
        return carry

    lax.fori_loop(0, t_all // LRU_CHUNK, chunk_body, 0)

    half = SUBLANES * SEG_LEN

    def seg_rows(j, hi):
        return pl.ds(j + hi * half, SUBLANES, stride=SEG_LEN)

    def pass1(j, st):
        hf0, hf1, pf0, pf1, hb0, hb1, pb0, pb1 = st
        jb = SEG_LEN - 1 - j
        a0 = af_ref[seg_rows(j, 0), :]
        a1 = af_ref[seg_rows(j, 1), :]
        hf0 = a0 * hf0 + bf_ref[seg_rows(j, 0), :]
        hf1 = a1 * hf1 + bf_ref[seg_rows(j, 1), :]
        pf0 = a0 * pf0
        pf1 = a1 * pf1
        c0 = ab_ref[seg_rows(jb, 0), :]
        c1 = ab_ref[seg_rows(jb, 1), :]
        hb0 = c0 * hb0 + bb_ref[seg_rows(jb, 0), :]
        hb1 = c1 * hb1 + bb_ref[seg_rows(jb, 1), :]
        pb0 = c0 * pb0
        pb1 = c1 * pb1
        return hf0, hf1, pf0, pf1, hb0, hb1, pb0, pb1

    one8 = jnp.ones((SUBLANES, LANES), f32)
    st = lax.fori_loop(0, SEG_LEN, pass1, (zero8, zero8, one8, one8, zero8, zero8, one8, one8))
    for idx, v in enumerate(st):
        ends_ref[idx * SUBLANES:(idx + 1) * SUBLANES, :] = v

    c = jnp.zeros((1, LANES), f32)
    for s in range(N_SEG):
        carry_ref[s:s + 1, :] = c
        c = ends_ref[s:s + 1, :] + ends_ref[N_SEG + s:N_SEG + s + 1, :] * c
    c = jnp.zeros((1, LANES), f32)
    for s in range(N_SEG - 1, -1, -1):
        carry_ref[N_SEG + s:N_SEG + s + 1, :] = c
        c = ends_ref[2 * N_SEG + s:2 * N_SEG + s + 1, :] + ends_ref[3 * N_SEG + s:3 * N_SEG + s + 1, :] * c

    def pass2(j, st):
        hf0, hf1, hb0, hb1 = st
        jb = SEG_LEN - 1 - j
        hf0 = af_ref[seg_rows(j, 0), :] * hf0 + bf_ref[seg_rows(j, 0), :]
        hf1 = af_ref[seg_rows(j, 1), :] * hf1 + bf_ref[seg_rows(j, 1), :]
        bf_ref[seg_rows(j, 0), :] = hf0
        bf_ref[seg_rows(j, 1), :] = hf1
        hb0 = ab_ref[seg_rows(jb, 0), :] * hb0 + bb_ref[seg_rows(jb, 0), :]
        hb1 = ab_ref[seg_rows(jb, 1), :] * hb1 + bb_ref[seg_rows(jb, 1), :]
        bb_ref[seg_rows(jb, 0), :] = hb0
        bb_ref[seg_rows(jb, 1), :] = hb1
        return hf0, hf1, hb0, hb1

    lax.fori_loop(0, SEG_LEN, pass2,
                  (carry_ref[0:SUBLANES, :], carry_ref[SUBLANES:2 * SUBLANES, :],
                   carry_ref[2 * SUBLANES:3 * SUBLANES, :], carry_ref[3 * SUBLANES:4 * SUBLANES, :]))

    o_ref[...] = bf_ref[N_META:t_all, :] + bb_ref[N_META:t_all, :]


def _lru(lru_x, lru_meta, conv_w, conv_b, wg, bg, lam, batch, seq):
    t_all = N_META + seq
    n_slab = LRU_WIDTH // LANES
    slab = lambda r: pl.BlockSpec((r, LANES), lambda b, s: (0, s))
    return pl.pallas_call(
        _lru_kernel,
        grid=(batch, n_slab),
        in_specs=[pl.BlockSpec((None, seq, LANES), lambda b, s: (b, 0, s)),
                  slab(N_META), slab(4), slab(1),
                  pl.BlockSpec((None, LANES, 4 * LANES), lambda b, s: (s, 0, 0)),
                  pl.BlockSpec((None, 1, 4 * LANES), lambda b, s: (s, 0, 0)),
                  slab(2)],
        out_specs=pl.BlockSpec((None, seq, LANES), lambda b, s: (b, 0, s)),
        out_shape=jax.ShapeDtypeStruct((batch, seq, LRU_WIDTH), f32),
        scratch_shapes=[pltpu.VMEM((t_all + 2 * SUBLANES, LANES), f32)]
                       + [pltpu.VMEM((t_all, LANES), f32)] * 4
                       + [pltpu.VMEM((4 * N_SEG, LANES), f32), pltpu.VMEM((2 * N_SEG, LANES), f32)],
        compiler_params=pltpu.CompilerParams(dimension_semantics=("arbitrary", "arbitrary"),
                                             vmem_limit_bytes=VMEM_LIMIT),
        name="lru",
    )(lru_x, lru_meta, conv_w, conv_b, wg, bg, lam)


def _cmul_const(x, c, s):
    re, im = x
    tol = 1e-12
    if abs(s) < tol:
        return (re, im) if c > 0 else (-re, -im)
    if abs(c) < tol:
        return (-im, re) if s > 0 else (im, -re)
    if abs(abs(c) - abs(s)) < tol:
        m = abs(c)
        sc, ss = (1.0 if c > 0 else -1.0), (1.0 if s > 0 else -1.0)
        return (m * (sc * re - ss * im), m * (ss * re + sc * im))
    return (c * re - s * im, s * re + c * im)


def _fft(xs):
    n = len(xs)
    if n == 1:
        return xs
    ev = _fft(xs[0::2])
    od = _fft(xs[1::2])
    out = [None] * n
    for k in range(n // 2):
        ang = -2.0 * math.pi * k / n
        t = _cmul_const(od[k], math.cos(ang), math.sin(ang))
        out[k] = (ev[k][0] + t[0], ev[k][1] + t[1])
        out[k + n // 2] = (ev[k][0] - t[0], ev[k][1] - t[1])
    return out


def _fourier_kernel(x_ref, m_ref, fm_ref, twc_ref, tws_ref, o_ref, y_ref):
    def stage1(t2, carry):
        g = jnp.dot(fm_ref[...], x_ref[t2], preferred_element_type=f32)
        m = m_ref[t2]
        ca = g[0:N1_PAD, 0:LANES]
        cb = g[0:N1_PAD, LANES:2 * LANES]
        sa = g[N1_PAD:2 * N1_PAD, 0:LANES]
        sb = g[N1_PAD:2 * N1_PAD, LANES:2 * LANES]
        yr = ca - sb + m[:, 0:LANES]
        yi = -(cb + sa) - m[:, LANES:2 * LANES]
        cw = twc_ref[t2]
        sw = tws_ref[t2]
        y_ref[t2, 0] = yr * cw + yi * sw
        y_ref[t2, 1] = yi * cw - yr * sw
        return carry

    lax.fori_loop(0, N2, stage1, 0)

    def stage2(r, carry):
        rows = pl.ds(pl.multiple_of(r * SUBLANES, SUBLANES), SUBLANES)
        ys = [(y_ref[t2, 0, rows, :], y_ref[t2, 1, rows, :]) for t2 in range(N2)]
        outs = _fft(ys)
        for k2 in range(N2):
            o_ref[k2, rows, :] = outs[k2][0]
        return carry

    lax.fori_loop(0, N1_PAD // SUBLANES, stage2, 0)


def _fourier(xr, mr, fm, twc, tws, batch):
    n_blk = FOURIER_WIDTH // LANES
    return pl.pallas_call(
        _fourier_kernel,
        grid=(batch, n_blk),
        in_specs=[pl.BlockSpec((None, None, N2, N1 - 1, 2 * LANES), lambda b, j: (b, j, 0, 0, 0)),
                  pl.BlockSpec((None, N2, 1, 2 * LANES), lambda b, j: (j, 0, 0, 0)),
                  _resident(fm.shape), _resident(twc.shape), _resident(tws.shape)],
        out_specs=pl.BlockSpec((None, N2, N1_PAD, LANES), lambda b, j: (b, 0, 0, j)),
        out_shape=jax.ShapeDtypeStruct((batch, N2, N1_PAD, FOURIER_WIDTH), f32),
        scratch_shapes=[pltpu.VMEM((N2, 2, N1_PAD, LANES), f32)],
        compiler_params=pltpu.CompilerParams(dimension_semantics=("arbitrary", "arbitrary"),
                                             vmem_limit_bytes=VMEM_LIMIT),
        name="fourier",
    )(xr, mr, fm, twc, tws)


def _fourier_constants():
    k1 = np.arange(N1_PAD)[:, None]
    t1 = np.arange(1, N1)[None, :]
    ang = 2.0 * np.pi * ((k1 * t1) % N1) / N1
    valid = (k1 < N1)
    fm = np.concatenate([np.cos(ang) * valid, np.sin(ang) * valid], axis=0)
    t2 = np.arange(N2)[:, None]
    k1r = np.arange(N1_PAD)[None, :]
    ang2 = 2.0 * np.pi * ((t2 * k1r) % (N1 * N2)) / (N1 * N2)
    valid2 = (k1r < N1)
    twc = (np.cos(ang2) * valid2)[:, :, None]
    tws = (np.sin(ang2) * valid2)[:, :, None]
    return (jnp.asarray(fm, f32), jnp.asarray(twc, f32), jnp.asarray(tws, f32))


def _mix_out_kernel(h1_ref, hs_ref, gate_ref, four_ref, gl_ref, fb_ref, gf_ref, wo_ref,
                    g2_ref, w1_ref, w2_ref, gfin_ref, o_ref, mix_ref, act_ref):
    y_l = _rms_norm(hs_ref[...] * _gelu_tanh(gate_ref[...]), gl_ref[...])
    y_f = _rms_norm(four_ref[...] + fb_ref[...], gf_ref[...])
    mix_ref[:, 0:LRU_WIDTH] = y_l.astype(bf16)
    mix_ref[:, LRU_WIDTH:] = y_f.astype(bf16)
    h2 = h1_ref[...] + jnp.dot(mix_ref[...], wo_ref[...], preferred_element_type=f32)
    xn = _rms_norm(h2, g2_ref[...]).astype(bf16)
    h3 = h2 + 0.5 * _swiglu(xn, w1_ref, w2_ref, act_ref)
    o_ref[...] = _rms_norm(h3, gfin_ref[...])


def _mix_out(h1, hs, gate, four, gl, fb, gf, wo, g2, w1, w2, gfin, tile):
    rows = h1.shape[0]
    row_spec = lambda w: pl.BlockSpec((tile, w), lambda i: (i, 0))
    return pl.pallas_call(
        _mix_out_kernel,
        grid=(rows // tile,),
        in_specs=[row_spec(D_MODEL), row_spec(LRU_WIDTH), row_spec(LRU_WIDTH), row_spec(FOURIER_WIDTH),
                  _resident((1, LRU_WIDTH)), _resident((1, FOURIER_WIDTH)), _resident((1, FOURIER_WIDTH)),
                  _resident(wo.shape), _resident((1, D_MODEL)), _resident(w1.shape), _resident(w2.shape),
                  _resident((1, D_MODEL))],
        out_specs=row_spec(D_MODEL),
        out_shape=jax.ShapeDtypeStruct((rows, D_MODEL), f32),
        scratch_shapes=[pltpu.VMEM((tile, D_MODEL), bf16), pltpu.VMEM((tile, D_FF), bf16)],
        compiler_params=pltpu.CompilerParams(dimension_semantics=("arbitrary",), vmem_limit_bytes=VMEM_LIMIT),
        name="mix_out",
    )(h1, hs, gate, four, gl, fb, gf, wo, g2, w1, w2, gfin)


def _gate_weights(wa_f, wx_f, wa_b, wx_b, ba_f, bx_f, ba_b, bx_b):
    n_slab = LRU_WIDTH // LANES
    per = LANES // LRU_HEAD_DIM
    eye = jnp.eye(per, dtype=f32)

    def slab_bd(w):
        w = w.astype(f32).reshape(n_slab, per, LRU_HEAD_DIM, LRU_HEAD_DIM)
        return (eye[None, :, None, :, None] * w[:, :, :, None, :]).reshape(n_slab, LANES, LANES)

    wg = jnp.concatenate([slab_bd(w) for w in (wa_f, wx_f, wa_b, wx_b)], axis=-1).astype(bf16)
    bg = jnp.concatenate([b.astype(f32).reshape(n_slab, 1, LANES) for b in (ba_f, bx_f, ba_b, bx_b)], axis=-1)
    return wg, bg


def kernel(x, meta_tokens, norm_ffn1, w_ffn1_in, w_ffn1_out, norm_mix, w_in, conv_w, conv_b, lru_wa_fwd, lru_ba_fwd, lru_wx_fwd, lru_bx_fwd, lru_lambda_fwd, lru_wa_bwd, lru_ba_bwd, lru_wx_bwd, lru_bx_bwd, lru_lambda_bwd, fourier_w, fourier_b, norm_lru_out, norm_fourier_out, w_out, norm_ffn2, w_ffn2_in, w_ffn2_out, norm_final):
    batch, seq, d = x.shape
    assert (d, seq + N_META) == (D_MODEL, N1 * N2) and norm_ffn1.shape[0] == 1
    l = 0
    row = lambda v: v.astype(f32).reshape(1, -1)

    w_four = _fold_fourier_weights(w_in[l][:, 2 * LRU_WIDTH:], fourier_w[l])
    win = jnp.concatenate([w_in[l][:, :2 * LRU_WIDTH].astype(f32), w_four], axis=1).astype(bf16)
    w1a, w2a = w_ffn1_in[l].astype(bf16), w_ffn1_out[l].astype(bf16)
    w1b, w2b = w_ffn2_in[l].astype(bf16), w_ffn2_out[l].astype(bf16)
    wo = w_out[l].astype(bf16)
    wg, bg = _gate_weights(lru_wa_fwd[l], lru_wx_fwd[l], lru_wa_bwd[l], lru_wx_bwd[l],
                           lru_ba_fwd[l], lru_bx_fwd[l], lru_ba_bwd[l], lru_bx_bwd[l])
    lam = jnp.stack([lru_lambda_fwd[l], lru_lambda_bwd[l]]).astype(f32)

    ffn_in = functools.partial(_ffn_in, g1=row(norm_ffn1[l]), w1=w1a, w2=w2a, g2=row(norm_mix[l]), win=win)
    h1, lru_x, gate, ab = ffn_in(x.reshape(batch * seq, d).astype(f32), tile=ROW_TILE)
    _, lru_m, _, ab_m = ffn_in(meta_tokens.astype(f32), tile=N_META)

    hsum = _lru(lru_x.reshape(batch, seq, LRU_WIDTH), lru_m, conv_w[l].astype(f32), row(conv_b[l]),
                wg, bg, lam, batch, seq)

    n_blk = FOURIER_WIDTH // LANES
    xr = ab.reshape(batch, N1 - 1, N2, 2, n_blk, LANES).transpose(0, 4, 2, 1, 3, 5)
    xr = xr.reshape(batch, n_blk, N2, N1 - 1, 2 * LANES)
    mr = ab_m.astype(f32).reshape(N2, 2, n_blk, LANES).transpose(2, 0, 1, 3).reshape(n_blk, N2, 1, 2 * LANES)
    fm, twc, tws = _fourier_constants()
    four = _fourier(xr, mr, fm.astype(bf16),
                    jnp.broadcast_to(twc, (N2, N1_PAD, LANES)), jnp.broadcast_to(tws, (N2, N1_PAD, LANES)), batch)
    four = four[:, :, :N1, :].reshape(batch, N1 * N2, FOURIER_WIDTH)[:, N_META:, :]

    out = _mix_out(h1, hsum.reshape(batch * seq, LRU_WIDTH), gate, four.reshape(batch * seq, FOURIER_WIDTH),
                   row(norm_lru_out[l]), row(fourier_b[l]), row(norm_fourier_out[l]), wo,
                   row(norm_ffn2[l]), w1b, w2b, row(norm_final), tile=ROW_TILE)
    return out.reshape(batch, seq, d).astype(x.dtype)
```

```python
import functools
import math

import numpy as np
import jax
import jax.numpy as jnp
from jax import lax
from jax.experimental import pallas as pl
from jax.experimental.pallas import tpu as pltpu

D_MODEL = 1024
N_META = 16
LRU_WIDTH = 512
LRU_HEADS = 8
LRU_HEAD_DIM = 64
FOURIER_WIDTH = 512
FOURIER_GROUPS = 8
FOURIER_GROUP_DIM = 64
D_FF = 2816
EPS = 1e-6
RG_LRU_C = 8.0

LANES = 128
SUBLANES = 8
ROW_TILE = 512
FF_CHUNK = 1408
VMEM_LIMIT = 60 * 1024 * 1024

N_SLAB = N_META
SLAB = 513
SLAB_PAD = 520
N_SEG = 16
SEG_LEN = 513
LRU_CHUNK = 456

f32 = jnp.float32
bf16 = jnp.bfloat16


def _rms_norm(x, g):
    return x * lax.rsqrt(jnp.mean(x * x, axis=-1, keepdims=True) + EPS) * g


def _sigmoid(x):
    return 0.5 * (1.0 + jnp.tanh(0.5 * x))


def _gelu_tanh(x):
    c = math.sqrt(2.0 / math.pi)
    return 0.5 * x * (1.0 + jnp.tanh(c * (x + 0.044715 * (x * x * x))))


def _swiglu(xn, w1_ref, w2_ref, act_ref):
    for c in range(D_FF // FF_CHUNK):
        lo = c * FF_CHUNK
        g = jnp.dot(xn, w1_ref[:, lo:lo + FF_CHUNK], preferred_element_type=f32)
        u = jnp.dot(xn, w1_ref[:, D_FF + lo:D_FF + lo + FF_CHUNK], preferred_element_type=f32)
        act_ref[:, lo:lo + FF_CHUNK] = (g * _sigmoid(g) * u).astype(bf16)
    return jnp.dot(act_ref[...], w2_ref[...], preferred_element_type=f32)


def _resident(shape):
    nd = len(shape)
    return pl.BlockSpec(shape, lambda *_: (0,) * nd, pipeline_mode=pl.Buffered(1))


def _fold_kernel(w4_ref, bd_ref, cd_ref, sd_ref, o_ref):
    hi = lax.Precision.HIGHEST
    mc = jnp.dot(cd_ref[...], bd_ref[...], preferred_element_type=f32, precision=hi)
    ms = jnp.dot(sd_ref[...], bd_ref[...], preferred_element_type=f32, precision=hi)
    w4 = w4_ref[...]
    o_ref[:, 0:FOURIER_WIDTH] = jnp.dot(w4, mc, preferred_element_type=f32, precision=hi)
    o_ref[:, FOURIER_WIDTH:] = jnp.dot(w4, ms, preferred_element_type=f32, precision=hi)


def _fold_fourier_weights(w_four_in, fourier_w):
    g, n = FOURIER_GROUPS, FOURIER_GROUP_DIM
    eye = np.eye(g, dtype=np.float64)
    ang = 2.0 * np.pi * (np.outer(np.arange(n), np.arange(n)) % n) / n
    scale = 1.0 / math.sqrt((N_SLAB * SLAB) * n)
    cd = jnp.asarray(np.kron(eye, np.cos(ang)) * scale, f32)
    sd = jnp.asarray(np.kron(eye, -np.sin(ang)) * scale, f32)
    bd = (jnp.eye(g, dtype=f32)[:, None, :, None] * fourier_w.astype(f32)[:, :, None, :]).reshape(g * n, g * n)
    return pl.pallas_call(
        _fold_kernel,
        out_shape=jax.ShapeDtypeStruct((D_MODEL, 2 * FOURIER_WIDTH), f32),
        name="fold_fourier",
    )(w_four_in.astype(f32), bd, cd, sd)


def _ffn_in_kernel(x_ref, g1_ref, w1_ref, w2_ref, g2_ref, win_ref,
                   h1_ref, lru_ref, gate_ref, ab_ref, act_ref):
    x = x_ref[...]
    xn = _rms_norm(x, g1_ref[...]).astype(bf16)
    h1 = x + 0.5 * _swiglu(xn, w1_ref, w2_ref, act_ref)
    h1_ref[...] = h1
    xn2 = _rms_norm(h1, g2_ref[...]).astype(bf16)
    u = jnp.dot(xn2, win_ref[...], preferred_element_type=f32)
    lru_ref[...] = u[:, 0:LRU_WIDTH]
    gate_ref[...] = u[:, LRU_WIDTH:2 * LRU_WIDTH]
    ab_ref[...] = u[:, 2 * LRU_WIDTH:]


def _ffn_in(x2d, g1, w1, w2, g2, win, tile):
    rows = x2d.shape[0]
    n_in = win.shape[1]
    row_spec = lambda w: pl.BlockSpec((tile, w), lambda i: (i, 0))
    return pl.pallas_call(
        _ffn_in_kernel,
        grid=(rows // tile,),
        in_specs=[row_spec(D_MODEL), _resident((1, D_MODEL)), _resident(w1.shape), _resident(w2.shape),
                  _resident((1, D_MODEL)), _resident(win.shape)],
        out_specs=[row_spec(D_MODEL), row_spec(LRU_WIDTH), row_spec(LRU_WIDTH), row_spec(n_in - 2 * LRU_WIDTH)],
        out_shape=[jax.ShapeDtypeStruct((rows, D_MODEL), f32),
                   jax.ShapeDtypeStruct((rows, LRU_WIDTH), f32),
                   jax.ShapeDtypeStruct((rows, LRU_WIDTH), f32),
                   jax.ShapeDtypeStruct((rows, n_in - 2 * LRU_WIDTH), f32)],
        scratch_shapes=[pltpu.VMEM((tile, D_FF), bf16)],
        compiler_params=pltpu.CompilerParams(dimension_semantics=("arbitrary",), vmem_limit_bytes=VMEM_LIMIT),
        name="ffn_in",
    )(x2d, g1, w1, w2, g2, win)


def _lru_kernel(xl_ref, xm_ref, cw_ref, cb_ref, wg_ref, bg_ref, lam_ref, o_ref,
                seq_ref, af_ref, bf_ref, ab_ref, bb_ref, ends_ref, carry_ref):
    t_all = N_SEG * SEG_LEN
    zero8 = jnp.zeros((SUBLANES, LANES), f32)
    seq_ref[0:SUBLANES, :] = zero8
    seq_ref[SUBLANES:SUBLANES + N_META, :] = xm_ref[...]
    seq_ref[SUBLANES + N_META:SUBLANES + t_all, :] = xl_ref[...]
    seq_ref[SUBLANES + t_all:2 * SUBLANES + t_all, :] = zero8

    cw = cw_ref[...]
    cb = cb_ref[...]
    bg = bg_ref[...]
    lam = lam_ref[...]
    nlam = -lam
    sp = jnp.maximum(nlam, 0.0) + jnp.log1p(jnp.exp(-jnp.abs(nlam)))
    sp_f = sp[0:1, :]
    sp_b = sp[1:2, :]

    def gates(pre_r, pre_i, xc, sp_dir):
        r = _sigmoid(pre_r)
        i = _sigmoid(pre_i)
        t = jnp.tanh((-0.5 * RG_LRU_C) * r * sp_dir)
        q = 1.0 / (1.0 - t)
        a = (1.0 + t) * q
        mult = 2.0 * jnp.sqrt(-t) * q
        return a, mult * (i * xc)

    def chunk_body(c, carry):
        r0 = pl.multiple_of(c * LRU_CHUNK, SUBLANES)
        win = seq_ref[pl.ds(r0, LRU_CHUNK + 2 * SUBLANES), :]
        xc = (cw[0:1, :] * win[6:6 + LRU_CHUNK] + cw[1:2, :] * win[7:7 + LRU_CHUNK]
              + cw[2:3, :] * win[8:8 + LRU_CHUNK] + cw[3:4, :] * win[9:9 + LRU_CHUNK] + cb)
        pre = jnp.dot(xc.astype(bf16), wg_ref[...], preferred_element_type=f32) + bg
        a_f, b_f = gates(pre[:, 0:LANES], pre[:, LANES:2 * LANES], xc, sp_f)
        a_b, b_b = gates(pre[:, 2 * LANES:3 * LANES], pre[:, 3 * LANES:4 * LANES], xc, sp_b)
        rows = pl.ds(r0, LRU_CHUNK)
        af_ref[rows, :] = a_f
        bf_ref[rows, :] = b_f
        ab_ref[rows, :] = a_b
        bb_ref[rows, :] = b_b
        return carry

    lax.fori_loop(0, t_all // LRU_CHUNK, chunk_body, 0)

    half = SUBLANES * SEG_LEN

    def seg_rows(j, hi):
        return pl.ds(j + hi * half, SUBLANES, stride=SEG_LEN)

    def pass1(j, st):
        hf0, hf1, pf0, pf1, hb0, hb1, pb0, pb1 = st
        jb = SEG_LEN - 1 - j
        a0 = af_ref[seg_rows(j, 0), :]
        a1 = af_ref[seg_rows(j, 1), :]
        hf0 = a0 * hf0 + bf_ref[seg_rows(j, 0), :]
        hf1 = a1 * hf1 + bf_ref[seg_rows(j, 1), :]
        pf0 = a0 * pf0
        pf1 = a1 * pf1
        c0 = ab_ref[seg_rows(jb, 0), :]
        c1 = ab_ref[seg_rows(jb, 1), :]
        hb0 = c0 * hb0 + bb_ref[seg_rows(jb, 0), :]
        hb1 = c1 * hb1 + bb_ref[seg_rows(jb, 1), :]
        pb0 = c0 * pb0
        pb1 = c1 * pb1
        return hf0, hf1, pf0, pf1, hb0, hb1, pb0, pb1

    one8 = jnp.ones((SUBLANES, LANES), f32)
    st = lax.fori_loop(0, SEG_LEN, pass1, (zero8, zero8, one8, one8, zero8, zero8, one8, one8))
    for idx, v in enumerate(st):
        ends_ref[idx * SUBLANES:(idx + 1) * SUBLANES, :] = v

    c = jnp.zeros((1, LANES), f32)
    for s in range(N_SEG):
        carry_ref[s:s + 1, :] = c
        c = ends_ref[s:s + 1, :] + ends_ref[N_SEG + s:N_SEG + s + 1, :] * c
    c = jnp.zeros((1, LANES), f32)
    for s in range(N_SEG - 1, -1, -1):
        carry_ref[N_SEG + s:N_SEG + s + 1, :] = c
        c = ends_ref[2 * N_SEG + s:2 * N_SEG + s + 1, :] + ends_ref[3 * N_SEG + s:3 * N_SEG + s + 1, :] * c

    def pass2(j, st):
        hf0, hf1, hb0, hb1 = st
        jb = SEG_LEN - 1 - j
        hf0 = af_ref[seg_rows(j, 0), :] * hf0 + bf_ref[seg_rows(j, 0), :]
        hf1 = af_ref[seg_rows(j, 1), :] * hf1 + bf_ref[seg_rows(j, 1), :]
        bf_ref[seg_rows(j, 0), :] = hf0
        bf_ref[seg_rows(j, 1), :] = hf1
        hb0 = ab_ref[seg_rows(jb, 0), :] * hb0 + bb_ref[seg_rows(jb, 0), :]
        hb1 = ab_ref[seg_rows(jb, 1), :] * hb1 + bb_ref[seg_rows(jb, 1), :]
        bb_ref[seg_rows(jb, 0), :] = hb0
        bb_ref[seg_rows(jb, 1), :] = hb1
        return hf0, hf1, hb0, hb1

    lax.fori_loop(0, SEG_LEN, pass2,
                  (carry_ref[0:SUBLANES, :], carry_ref[SUBLANES:2 * SUBLANES, :],
                   carry_ref[2 * SUBLANES:3 * SUBLANES, :], carry_ref[3 * SUBLANES:4 * SUBLANES, :]))

    o_ref[...] = bf_ref[N_META:t_all, :] + bb_ref[N_META:t_all, :]


def _lru(lru_x, lru_meta, conv_w, conv_b, wg, bg, lam, batch, seq):
    t_all = N_META + seq
    n_slab = LRU_WIDTH // LANES
    slab = lambda r: pl.BlockSpec((r, LANES), lambda b, s: (0, s))
    return pl.pallas_call(
        _lru_kernel,
        grid=(batch, n_slab),
        in_specs=[pl.BlockSpec((None, seq, LANES), lambda b, s: (b, 0, s)),
                  slab(N_META), slab(4), slab(1),
                  pl.BlockSpec((None, LANES, 4 * LANES), lambda b, s: (s, 0, 0)),
                  pl.BlockSpec((None, 1, 4 * LANES), lambda b, s: (s, 0, 0)),
                  slab(2)],
        out_specs=pl.BlockSpec((None, seq, LANES), lambda b, s: (b, 0, s)),
        out_shape=jax.ShapeDtypeStruct((batch, seq, LRU_WIDTH), f32),
        scratch_shapes=[pltpu.VMEM((t_all + 2 * SUBLANES, LANES), f32)]
                       + [pltpu.VMEM((t_all, LANES), f32)] * 4
                       + [pltpu.VMEM((4 * N_SEG, LANES), f32), pltpu.VMEM((2 * N_SEG, LANES), f32)],
        compiler_params=pltpu.CompilerParams(dimension_semantics=("arbitrary", "arbitrary"),
                                             vmem_limit_bytes=VMEM_LIMIT),
        name="lru",
    )(lru_x, lru_meta, conv_w, conv_b, wg, bg, lam)


def _cmul_const(x, c, s):
    re, im = x
    tol = 1e-12
    if abs(s) < tol:
        return (re, im) if c > 0 else (-re, -im)
    if abs(c) < tol:
        return (-im, re) if s > 0 else (im, -re)
    if abs(abs(c) - abs(s)) < tol:
        m = abs(c)
        sc, ss = (1.0 if c > 0 else -1.0), (1.0 if s > 0 else -1.0)
        return (m * (sc * re - ss * im), m * (ss * re + sc * im))
    return (c * re - s * im, s * re + c * im)


def _fft(xs):
    n = len(xs)
    if n == 1:
        return xs
    ev = _fft(xs[0::2])
    od = _fft(xs[1::2])
    out = [None] * n
    for k in range(n // 2):
        ang = -2.0 * math.pi * k / n
        t = _cmul_const(od[k], math.cos(ang), math.sin(ang))
        out[k] = (ev[k][0] + t[0], ev[k][1] + t[1])
        out[k + n // 2] = (ev[k][0] - t[0], ev[k][1] - t[1])
    return out


def _fourier_kernel(xa_ref, xb_ref, ma_ref, mb_ref, fm_ref, twc_ref, tws_ref, o_ref,
                    ha_ref, hb_ref, ta_ref, tb_ref, y_ref):
    tail0 = (N_SLAB - 1) * SLAB - N_META
    for m_ref, x_ref, h_ref, t_ref in ((ma_ref, xa_ref, ha_ref, ta_ref), (mb_ref, xb_ref, hb_ref, tb_ref)):
        h_ref[0:N_META, :] = m_ref[...]
        h_ref[N_META:SLAB_PAD, :] = x_ref[0:SLAB_PAD - N_META, :]
        t_ref[SLAB - 1:SLAB_PAD, :] = jnp.zeros((SLAB_PAD - SLAB + 1, LANES), f32)
        t_ref[0:SLAB - 1, :] = x_ref[tail0:tail0 + SLAB - 1, :]
        t_ref[SLAB - 1:SLAB, :] = x_ref[tail0 + SLAB - 1:tail0 + SLAB, :]

    def stage1(it, carry):
        r = pl.multiple_of(it * SUBLANES, SUBLANES)
        rows = pl.ds(r, SUBLANES)

        def slab_rows(x_ref, h_ref, t_ref, t1):
            if t1 == 0:
                return h_ref[rows, :]
            if t1 == N_SLAB - 1:
                return t_ref[rows, :]
            return x_ref[pl.ds(r + (t1 * SLAB - N_META), SUBLANES), :]

        zs = [(slab_rows(xa_ref, ha_ref, ta_ref, t1), slab_rows(xb_ref, hb_ref, tb_ref, t1))
              for t1 in range(N_SLAB)]
        ys = _fft(zs)
        for k1 in range(N_SLAB):
            yr, yi = ys[k1]
            if k1 > 0:
                cw = twc_ref[k1, rows, :]
                sw = tws_ref[k1, rows, :]
                yr, yi = yr * cw + yi * sw, yi * cw - yr * sw
            y_ref[rows, k1 * LANES:(k1 + 1) * LANES] = yr
            y_ref[pl.ds(r + SLAB_PAD, SUBLANES), k1 * LANES:(k1 + 1) * LANES] = yi
        return carry

    lax.fori_loop(0, SLAB_PAD // SUBLANES, stage1, 0)

    for p in range(N_SLAB // 2):
        yp = y_ref[:, 2 * p * LANES:(2 * p + 2) * LANES].astype(bf16)
        g = jnp.dot(fm_ref[...], yp, preferred_element_type=f32)
        o_ref[pl.ds(2 * p, SLAB - 1, stride=N_SLAB), :] = g[:, 0:LANES]
        o_ref[pl.ds(2 * p + 1, SLAB - 1, stride=N_SLAB), :] = g[:, LANES:2 * LANES]


def _fourier(ab, ab_m, fm, twc, tws, batch, seq):
    n_blk = FOURIER_WIDTH // LANES
    x_spec = lambda off: pl.BlockSpec((None, seq, LANES), lambda b, j: (b, 0, j + off))
    m_spec = lambda off: pl.BlockSpec((N_META, LANES), lambda b, j: (0, j + off))
    return pl.pallas_call(
        _fourier_kernel,
        grid=(batch, n_blk),
        in_specs=[x_spec(0), x_spec(n_blk), m_spec(0), m_spec(n_blk),
                  _resident(fm.shape), _resident(twc.shape), _resident(tws.shape)],
        out_specs=pl.BlockSpec((None, seq, LANES), lambda b, j: (b, 0, j)),
        out_shape=jax.ShapeDtypeStruct((batch, seq, FOURIER_WIDTH), f32),
        scratch_shapes=[pltpu.VMEM((SLAB_PAD, LANES), f32)] * 4
                       + [pltpu.VMEM((2 * SLAB_PAD, N_SLAB * LANES), f32)],
        compiler_params=pltpu.CompilerParams(dimension_semantics=("arbitrary", "arbitrary"),
                                             vmem_limit_bytes=VMEM_LIMIT),
        name="fourier",
    )(ab, ab, ab_m, ab_m, fm, twc, tws)


def _fourier_constants():
    n = N_SLAB * SLAB
    k2 = np.arange(1, SLAB)[:, None]
    t2 = np.arange(SLAB_PAD)[None, :]
    ang = 2.0 * np.pi * ((k2 * t2) % SLAB) / SLAB
    valid = (t2 < SLAB)
    fm = np.concatenate([np.cos(ang) * valid, np.sin(ang) * valid], axis=1)
    k1 = np.arange(N_SLAB)[:, None]
    ang2 = 2.0 * np.pi * ((k1 * t2) % n) / n
    twc = (np.cos(ang2) * valid)[:, :, None]
    tws = (np.sin(ang2) * valid)[:, :, None]
    return (jnp.asarray(fm, f32), jnp.asarray(twc, f32), jnp.asarray(tws, f32))


def _mix_out_kernel(h1_ref, hs_ref, gate_ref, four_ref, gl_ref, fb_ref, gf_ref, wo_ref,
                    g2_ref, w1_ref, w2_ref, gfin_ref, o_ref, mix_ref, act_ref):
    y_l = _rms_norm(hs_ref[...] * _gelu_tanh(gate_ref[...]), gl_ref[...])
    y_f = _rms_norm(four_ref[...] + fb_ref[...], gf_ref[...])
    mix_ref[:, 0:LRU_WIDTH] = y_l.astype(bf16)
    mix_ref[:, LRU_WIDTH:] = y_f.astype(bf16)
    h2 = h1_ref[...] + jnp.dot(mix_ref[...], wo_ref[...], preferred_element_type=f32)
    xn = _rms_norm(h2, g2_ref[...]).astype(bf16)
    h3 = h2 + 0.5 * _swiglu(xn, w1_ref, w2_ref, act_ref)
    o_ref[...] = _rms_norm(h3, gfin_ref[...])


def _mix_out(h1, hs, gate, four, gl, fb, gf, wo, g2, w1, w2, gfin, tile):
    rows = h1.shape[0]
    row_spec = lambda w: pl.BlockSpec((tile, w), lambda i: (i, 0))
    return pl.pallas_call(
        _mix_out_kernel,
        grid=(rows // tile,),
        in_specs=[row_spec(D_MODEL), row_spec(LRU_WIDTH), row_spec(LRU_WIDTH), row_spec(FOURIER_WIDTH),
                  _resident((1, LRU_WIDTH)), _resident((1, FOURIER_WIDTH)), _resident((1, FOURIER_WIDTH)),
                  _resident(wo.shape), _resident((1, D_MODEL)), _resident(w1.shape), _resident(w2.shape),
                  _resident((1, D_MODEL))],
        out_specs=row_spec(D_MODEL),
        out_shape=jax.ShapeDtypeStruct((rows, D_MODEL), f32),
        scratch_shapes=[pltpu.VMEM((tile, D_MODEL), bf16), pltpu.VMEM((tile, D_FF), bf16)],
        compiler_params=pltpu.CompilerParams(dimension_semantics=("arbitrary",), vmem_limit_bytes=VMEM_LIMIT),
        name="mix_out",
    )(h1, hs, gate, four, gl, fb, gf, wo, g2, w1, w2, gfin)


def _gate_weights(wa_f, wx_f, wa_b, wx_b, ba_f, bx_f, ba_b, bx_b):
    n_slab = LRU_WIDTH // LANES
    per = LANES // LRU_HEAD_DIM
    eye = jnp.eye(per, dtype=f32)

    def slab_bd(w):
        w = w.astype(f32).reshape(n_slab, per, LRU_HEAD_DIM, LRU_HEAD_DIM)
        return (eye[None, :, None, :, None] * w[:, :, :, None, :]).reshape(n_slab, LANES, LANES)

    wg = jnp.concatenate([slab_bd(w) for w in (wa_f, wx_f, wa_b, wx_b)], axis=-1).astype(bf16)
    bg = jnp.concatenate([b.astype(f32).reshape(n_slab, 1, LANES) for b in (ba_f, bx_f, ba_b, bx_b)], axis=-1)
    return wg, bg


def kernel(x, meta_tokens, norm_ffn1, w_ffn1_in, w_ffn1_out, norm_mix, w_in, conv_w, conv_b, lru_wa_fwd, lru_ba_fwd, lru_wx_fwd, lru_bx_fwd, lru_lambda_fwd, lru_wa_bwd, lru_ba_bwd, lru_wx_bwd, lru_bx_bwd, lru_lambda_bwd, fourier_w, fourier_b, norm_lru_out, norm_fourier_out, w_out, norm_ffn2, w_ffn2_in, w_ffn2_out, norm_final):
    batch, seq, d = x.shape
    assert (d, seq + N_META) == (D_MODEL, N_SLAB * SLAB) and norm_ffn1.shape[0] == 1
    l = 0
    row = lambda v: v.astype(f32).reshape(1, -1)

    w_four = _fold_fourier_weights(w_in[l][:, 2 * LRU_WIDTH:], fourier_w[l])
    win = jnp.concatenate([w_in[l][:, :2 * LRU_WIDTH].astype(f32), w_four], axis=1).astype(bf16)
    w1a, w2a = w_ffn1_in[l].astype(bf16), w_ffn1_out[l].astype(bf16)
    w1b, w2b = w_ffn2_in[l].astype(bf16), w_ffn2_out[l].astype(bf16)
    wo = w_out[l].astype(bf16)
    wg, bg = _gate_weights(lru_wa_fwd[l], lru_wx_fwd[l], lru_wa_bwd[l], lru_wx_bwd[l],
                           lru_ba_fwd[l], lru_bx_fwd[l], lru_ba_bwd[l], lru_bx_bwd[l])
    lam = jnp.stack([lru_lambda_fwd[l], lru_lambda_bwd[l]]).astype(f32)

    ffn_in = functools.partial(_ffn_in, g1=row(norm_ffn1[l]), w1=w1a, w2=w2a, g2=row(norm_mix[l]), win=win)
    h1, lru_x, gate, ab = ffn_in(x.reshape(batch * seq, d).astype(f32), tile=ROW_TILE)
    _, lru_m, _, ab_m = ffn_in(meta_tokens.astype(f32), tile=N_META)

    hsum = _lru(lru_x.reshape(batch, seq, LRU_WIDTH), lru_m, conv_w[l].astype(f32), row(conv_b[l]),
                wg, bg, lam, batch, seq)

    fm, twc, tws = _fourier_constants()
    four = _fourier(ab.reshape(batch, seq, 2 * FOURIER_WIDTH), ab_m, fm.astype(bf16),
                    jnp.broadcast_to(twc, (N_SLAB, SLAB_PAD, LANES)),
                    jnp.broadcast_to(tws, (N_SLAB, SLAB_PAD, LANES)), batch, seq)

    out = _mix_out(h1, hsum.reshape(batch * seq, LRU_WIDTH), gate, four.reshape(batch * seq, FOURIER_WIDTH),
                   row(norm_lru_out[l]), row(fourier_b[l]), row(norm_fourier_out[l]), wo,
                   row(norm_ffn2[l]), w1b, w2b, row(norm_final), tile=ROW_TILE)
    return out.reshape(batch, seq, d).astype(x.dtype)
```

```python
import functools
import math

import numpy as np
import jax
import jax.numpy as jnp
from jax import lax
from jax.experimental import pallas as pl
from jax.experimental.pallas import tpu as pltpu

D_MODEL = 1024
N_META = 16
LRU_WIDTH = 512
LRU_HEADS = 8
LRU_HEAD_DIM = 64
FOURIER_WIDTH = 512
FOURIER_GROUPS = 8
FOURIER_GROUP_DIM = 64
D_FF = 2816
EPS = 1e-6
RG_LRU_C = 8.0

LANES = 128
SUBLANES = 8
ROW_TILE = 512
MXU_COLS = 256
FF_CHUNKS = (5 * MXU_COLS, 6 * MXU_COLS)
VMEM_LIMIT = 60 * 1024 * 1024

N_SLAB = N_META
SLAB = 513
SLAB_PAD = 520
N_SEG = 16
SEG_LEN = 513
LRU_CHUNK = 456
SCAN_UNROLL = 9

f32 = jnp.float32
bf16 = jnp.bfloat16


def _rms_norm(x, g):
    return x * lax.rsqrt(jnp.mean(x * x, axis=-1, keepdims=True) + EPS) * g


def _sigmoid(x):
    return 0.5 * (1.0 + jnp.tanh(0.5 * x))


def _gelu_tanh(x):
    c = math.sqrt(2.0 / math.pi)
    return 0.5 * x * (1.0 + jnp.tanh(c * (x + 0.044715 * (x * x * x))))


def _swiglu(xn, w1_ref, w2_ref, act_ref):
    assert sum(FF_CHUNKS) == D_FF
    lo = 0
    for width in FF_CHUNKS:
        g = jnp.dot(xn, w1_ref[:, lo:lo + width], preferred_element_type=f32)
        u = jnp.dot(xn, w1_ref[:, D_FF + lo:D_FF + lo + width], preferred_element_type=f32)
        act_ref[:, lo:lo + width] = (g * _sigmoid(g) * u).astype(bf16)
        lo += width
    return jnp.dot(act_ref[...], w2_ref[...], preferred_element_type=f32)


def _resident(shape):
    nd = len(shape)
    return pl.BlockSpec(shape, lambda *_: (0,) * nd, pipeline_mode=pl.Buffered(1))


def _fold_kernel(w4_ref, bd_ref, cd_ref, sd_ref, o_ref):
    hi = lax.Precision.HIGHEST
    mc = jnp.dot(cd_ref[...], bd_ref[...], preferred_element_type=f32, precision=hi)
    ms = jnp.dot(sd_ref[...], bd_ref[...], preferred_element_type=f32, precision=hi)
    w4 = w4_ref[...]
    o_ref[:, 0:FOURIER_WIDTH] = jnp.dot(w4, mc, preferred_element_type=f32, precision=hi)
    o_ref[:, FOURIER_WIDTH:] = jnp.dot(w4, ms, preferred_element_type=f32, precision=hi)


def _fold_fourier_weights(w_four_in, fourier_w):
    g, n = FOURIER_GROUPS, FOURIER_GROUP_DIM
    eye = np.eye(g, dtype=np.float64)
    ang = 2.0 * np.pi * (np.outer(np.arange(n), np.arange(n)) % n) / n
    scale = 1.0 / math.sqrt((N_SLAB * SLAB) * n)
    cd = jnp.asarray(np.kron(eye, np.cos(ang)) * scale, f32)
    sd = jnp.asarray(np.kron(eye, -np.sin(ang)) * scale, f32)
    bd = (jnp.eye(g, dtype=f32)[:, None, :, None] * fourier_w.astype(f32)[:, :, None, :]).reshape(g * n, g * n)
    return pl.pallas_call(
        _fold_kernel,
        out_shape=jax.ShapeDtypeStruct((D_MODEL, 2 * FOURIER_WIDTH), f32),
        name="fold_fourier",
    )(w_four_in.astype(f32), bd, cd, sd)


def _ffn_in_kernel(x_ref, g1_ref, w1_ref, w2_ref, g2_ref, win_ref,
                   h1_ref, lru_ref, gate_ref, ab_ref, act_ref):
    x = x_ref[...]
    xn = _rms_norm(x, g1_ref[...]).astype(bf16)
    h1 = x + 0.5 * _swiglu(xn, w1_ref, w2_ref, act_ref)
    h1_ref[...] = h1
    xn2 = _rms_norm(h1, g2_ref[...]).astype(bf16)
    u = jnp.dot(xn2, win_ref[...], preferred_element_type=f32)
    lru_ref[...] = u[:, 0:LRU_WIDTH]
    gate_ref[...] = u[:, LRU_WIDTH:2 * LRU_WIDTH]
    ab_ref[...] = u[:, 2 * LRU_WIDTH:]


def _ffn_in(x2d, g1, w1, w2, g2, win, tile):
    rows = x2d.shape[0]
    n_in = win.shape[1]
    row_spec = lambda w: pl.BlockSpec((tile, w), lambda i: (i, 0))
    return pl.pallas_call(
        _ffn_in_kernel,
        grid=(rows // tile,),
        in_specs=[row_spec(D_MODEL), _resident((1, D_MODEL)), _resident(w1.shape), _resident(w2.shape),
                  _resident((1, D_MODEL)), _resident(win.shape)],
        out_specs=[row_spec(D_MODEL), row_spec(LRU_WIDTH), row_spec(LRU_WIDTH), row_spec(n_in - 2 * LRU_WIDTH)],
        out_shape=[jax.ShapeDtypeStruct((rows, D_MODEL), f32),
                   jax.ShapeDtypeStruct((rows, LRU_WIDTH), f32),
                   jax.ShapeDtypeStruct((rows, LRU_WIDTH), f32),
                   jax.ShapeDtypeStruct((rows, n_in - 2 * LRU_WIDTH), f32)],
        scratch_shapes=[pltpu.VMEM((tile, D_FF), bf16)],
        compiler_params=pltpu.CompilerParams(dimension_semantics=("arbitrary",), vmem_limit_bytes=VMEM_LIMIT),
        name="ffn_in",
    )(x2d, g1, w1, w2, g2, win)


def _lru_kernel(xl_ref, xm_ref, cw_ref, cb_ref, wg_ref, bg_ref, lam_ref, o_ref,
                seq_ref, af_ref, bf_ref, ab_ref, bb_ref, hf_ref, hb_ref, ends_ref, carry_ref):
    t_all = N_SEG * SEG_LEN
    zero8 = jnp.zeros((SUBLANES, LANES), f32)
    seq_ref[0:SUBLANES, :] = zero8
    seq_ref[SUBLANES:SUBLANES + N_META, :] = xm_ref[...]
    seq_ref[SUBLANES + N_META:SUBLANES + t_all, :] = xl_ref[...]
    seq_ref[SUBLANES + t_all:2 * SUBLANES + t_all, :] = zero8

    cw = cw_ref[...]
    cb = cb_ref[...]
    bg = bg_ref[...]
    lam = lam_ref[...]
    nlam = -lam
    sp = jnp.maximum(nlam, 0.0) + jnp.log1p(jnp.exp(-jnp.abs(nlam)))
    k_f = (-0.25 * RG_LRU_C) * sp[0:1, :]
    k_b = (-0.25 * RG_LRU_C) * sp[1:2, :]

    def gates(half_pre_r, half_pre_i, xc, k_dir):
        t = jnp.tanh(k_dir * jnp.tanh(half_pre_r) + k_dir)
        q = 1.0 / (1.0 - t)
        a = (1.0 + t) * q
        return a, (jnp.sqrt(-t) * q) * ((1.0 + jnp.tanh(half_pre_i)) * xc)

    def chunk_body(c, carry):
        r0 = pl.multiple_of(c * LRU_CHUNK, SUBLANES)
        xc = (cw[0:1, :] * seq_ref[pl.ds(r0 + 6, LRU_CHUNK), :] + cw[1:2, :] * seq_ref[pl.ds(r0 + 7, LRU_CHUNK), :]
              + cw[2:3, :] * seq_ref[pl.ds(r0 + 8, LRU_CHUNK), :] + cw[3:4, :] * seq_ref[pl.ds(r0 + 9, LRU_CHUNK), :]
              + cb)
        pre = jnp.dot(xc.astype(bf16), wg_ref[...], preferred_element_type=f32) + bg
        a_f, b_f = gates(pre[:, 0:LANES], pre[:, LANES:2 * LANES], xc, k_f)
        a_b, b_b = gates(pre[:, 2 * LANES:3 * LANES], pre[:, 3 * LANES:4 * LANES], xc, k_b)
        rows = pl.ds(r0, LRU_CHUNK)
        af_ref[rows, :] = a_f
        bf_ref[rows, :] = b_f
        ab_ref[rows, :] = a_b
        bb_ref[rows, :] = b_b
        return carry

    lax.fori_loop(0, t_all // LRU_CHUNK, chunk_body, 0)

    half = SUBLANES * SEG_LEN

    def seg_rows(j, hi):
        return pl.ds(j + hi * half, SUBLANES, stride=SEG_LEN)

    def pass1(j, st):
        hf0, hf1, pf0, pf1, hb0, hb1, pb0, pb1 = st
        jb = SEG_LEN - 1 - j
        a0 = af_ref[seg_rows(j, 0), :]
        a1 = af_ref[seg_rows(j, 1), :]
        hf0 = a0 * hf0 + bf_ref[seg_rows(j, 0), :]
        hf1 = a1 * hf1 + bf_ref[seg_rows(j, 1), :]
        pf0 = a0 * pf0
        pf1 = a1 * pf1
        c0 = ab_ref[seg_rows(jb, 0), :]
        c1 = ab_ref[seg_rows(jb, 1), :]
        hb0 = c0 * hb0 + bb_ref[seg_rows(jb, 0), :]
        hb1 = c1 * hb1 + bb_ref[seg_rows(jb, 1), :]
        pb0 = c0 * pb0
        pb1 = c1 * pb1
        return hf0, hf1, pf0, pf1, hb0, hb1, pb0, pb1

    one8 = jnp.ones((SUBLANES, LANES), f32)
    st = lax.fori_loop(0, SEG_LEN, pass1, (zero8, zero8, one8, one8, zero8, zero8, one8, one8),
                       unroll=SCAN_UNROLL)
    for idx, v in enumerate(st):
        ends_ref[idx * SUBLANES:(idx + 1) * SUBLANES, :] = v

    c = jnp.zeros((1, LANES), f32)
    for s in range(N_SEG):
        carry_ref[s:s + 1, :] = c
        c = ends_ref[s:s + 1, :] + ends_ref[N_SEG + s:N_SEG + s + 1, :] * c
    c = jnp.zeros((1, LANES), f32)
    for s in range(N_SEG - 1, -1, -1):
        carry_ref[N_SEG + s:N_SEG + s + 1, :] = c
        c = ends_ref[2 * N_SEG + s:2 * N_SEG + s + 1, :] + ends_ref[3 * N_SEG + s:3 * N_SEG + s + 1, :] * c

    def pass2(j, st):
        hf0, hf1, hb0, hb1 = st
        jb = SEG_LEN - 1 - j
        hf0 = af_ref[seg_rows(j, 0), :] * hf0 + bf_ref[seg_rows(j, 0), :]
        hf1 = af_ref[seg_rows(j, 1), :] * hf1 + bf_ref[seg_rows(j, 1), :]
        hf_ref[seg_rows(j, 0), :] = hf0
        hf_ref[seg_rows(j, 1), :] = hf1
        hb0 = ab_ref[seg_rows(jb, 0), :] * hb0 + bb_ref[seg_rows(jb, 0), :]
        hb1 = ab_ref[seg_rows(jb, 1), :] * hb1 + bb_ref[seg_rows(jb, 1), :]
        hb_ref[seg_rows(jb, 0), :] = hb0
        hb_ref[seg_rows(jb, 1), :] = hb1
        return hf0, hf1, hb0, hb1

    lax.fori_loop(0, SEG_LEN, pass2,
                  (carry_ref[0:SUBLANES, :], carry_ref[SUBLANES:2 * SUBLANES, :],
                   carry_ref[2 * SUBLANES:3 * SUBLANES, :], carry_ref[3 * SUBLANES:4 * SUBLANES, :]),
                  unroll=SCAN_UNROLL)

    o_ref[...] = hf_ref[N_META:t_all, :] + hb_ref[N_META:t_all, :]


def _lru(lru_x, lru_meta, conv_w, conv_b, wg, bg, lam, batch, seq):
    t_all = N_META + seq
    n_slab = LRU_WIDTH // LANES
    slab = lambda r: pl.BlockSpec((r, LANES), lambda b, s: (0, s))
    return pl.pallas_call(
        _lru_kernel,
        grid=(batch, n_slab),
        in_specs=[pl.BlockSpec((None, seq, LANES), lambda b, s: (b, 0, s)),
                  slab(N_META), slab(4), slab(1),
                  pl.BlockSpec((None, LANES, 4 * LANES), lambda b, s: (s, 0, 0)),
                  pl.BlockSpec((None, 1, 4 * LANES), lambda b, s: (s, 0, 0)),
                  slab(2)],
        out_specs=pl.BlockSpec((None, seq, LANES), lambda b, s: (b, 0, s)),
        out_shape=jax.ShapeDtypeStruct((batch, seq, LRU_WIDTH), f32),
        scratch_shapes=[pltpu.VMEM((t_all + 2 * SUBLANES, LANES), f32)]
                       + [pltpu.VMEM((t_all, LANES), f32)] * 6
                       + [pltpu.VMEM((4 * N_SEG, LANES), f32), pltpu.VMEM((2 * N_SEG, LANES), f32)],
        compiler_params=pltpu.CompilerParams(dimension_semantics=("arbitrary", "arbitrary"),
                                             vmem_limit_bytes=VMEM_LIMIT),
        name="lru",
    )(lru_x, lru_meta, conv_w, conv_b, wg, bg, lam)


def _cmul_const(x, c, s):
    re, im = x
    tol = 1e-12
    if abs(s) < tol:
        return (re, im) if c > 0 else (-re, -im)
    if abs(c) < tol:
        return (-im, re) if s > 0 else (im, -re)
    if abs(abs(c) - abs(s)) < tol:
        m = abs(c)
        sc, ss = (1.0 if c > 0 else -1.0), (1.0 if s > 0 else -1.0)
        return (m * (sc * re - ss * im), m * (ss * re + sc * im))
    return (c * re - s * im, s * re + c * im)


def _fft(xs):
    n = len(xs)
    if n == 1:
        return xs
    ev = _fft(xs[0::2])
    od = _fft(xs[1::2])
    out = [None] * n
    for k in range(n // 2):
        ang = -2.0 * math.pi * k / n
        t = _cmul_const(od[k], math.cos(ang), math.sin(ang))
        out[k] = (ev[k][0] + t[0], ev[k][1] + t[1])
        out[k + n // 2] = (ev[k][0] - t[0], ev[k][1] - t[1])
    return out


def _fourier_kernel(xa_ref, xb_ref, ma_ref, mb_ref, fm_ref, twc_ref, tws_ref, o_ref,
                    ha_ref, hb_ref, ta_ref, tb_ref, y_ref):
    tail0 = (N_SLAB - 1) * SLAB - N_META
    for m_ref, x_ref, h_ref, t_ref in ((ma_ref, xa_ref, ha_ref, ta_ref), (mb_ref, xb_ref, hb_ref, tb_ref)):
        h_ref[0:N_META, :] = m_ref[...]
        h_ref[N_META:SLAB_PAD, :] = x_ref[0:SLAB_PAD - N_META, :]
        t_ref[SLAB - 1:SLAB_PAD, :] = jnp.zeros((SLAB_PAD - SLAB + 1, LANES), f32)
        t_ref[0:SLAB - 1, :] = x_ref[tail0:tail0 + SLAB - 1, :]
        t_ref[SLAB - 1:SLAB, :] = x_ref[tail0 + SLAB - 1:tail0 + SLAB, :]

    def stage1(it, carry):
        r = pl.multiple_of(it * SUBLANES, SUBLANES)
        rows = pl.ds(r, SUBLANES)

        def slab_rows(x_ref, h_ref, t_ref, t1):
            if t1 == 0:
                return h_ref[rows, :]
            if t1 == N_SLAB - 1:
                return t_ref[rows, :]
            return x_ref[pl.ds(r + (t1 * SLAB - N_META), SUBLANES), :]

        zs = [(slab_rows(xa_ref, ha_ref, ta_ref, t1), slab_rows(xb_ref, hb_ref, tb_ref, t1))
              for t1 in range(N_SLAB)]
        ys = _fft(zs)
        for k1 in range(N_SLAB):
            yr, yi = ys[k1]
            if k1 > 0:
                cw = twc_ref[k1, rows, :]
                sw = tws_ref[k1, rows, :]
                yr, yi = yr * cw + yi * sw, yi * cw - yr * sw
            y_ref[rows, k1 * LANES:(k1 + 1) * LANES] = yr
            y_ref[pl.ds(r + SLAB_PAD, SUBLANES), k1 * LANES:(k1 + 1) * LANES] = yi
        return carry

    lax.fori_loop(0, SLAB_PAD // SUBLANES, stage1, 0)

    for p in range(N_SLAB // 2):
        yp = y_ref[:, 2 * p * LANES:(2 * p + 2) * LANES].astype(bf16)
        g = jnp.dot(fm_ref[...], yp, preferred_element_type=f32)
        o_ref[pl.ds(2 * p, SLAB - 1, stride=N_SLAB), :] = g[:, 0:LANES]
        o_ref[pl.ds(2 * p + 1, SLAB - 1, stride=N_SLAB), :] = g[:, LANES:2 * LANES]


def _fourier(ab, ab_m, fm, twc, tws, batch, seq):
    n_blk = FOURIER_WIDTH // LANES
    x_spec = lambda off: pl.BlockSpec((None, seq, LANES), lambda b, j: (b, 0, j + off))
    m_spec = lambda off: pl.BlockSpec((N_META, LANES), lambda b, j: (0, j + off))
    return pl.pallas_call(
        _fourier_kernel,
        grid=(batch, n_blk),
        in_specs=[x_spec(0), x_spec(n_blk), m_spec(0), m_spec(n_blk),
                  _resident(fm.shape), _resident(twc.shape), _resident(tws.shape)],
        out_specs=pl.BlockSpec((None, seq, LANES), lambda b, j: (b, 0, j)),
        out_shape=jax.ShapeDtypeStruct((batch, seq, FOURIER_WIDTH), f32),
        scratch_shapes=[pltpu.VMEM((SLAB_PAD, LANES), f32)] * 4
                       + [pltpu.VMEM((2 * SLAB_PAD, N_SLAB * LANES), f32)],
        compiler_params=pltpu.CompilerParams(dimension_semantics=("arbitrary", "arbitrary"),
                                             vmem_limit_bytes=VMEM_LIMIT),
        name="fourier",
    )(ab, ab, ab_m, ab_m, fm, twc, tws)


def _fourier_constants():
    n = N_SLAB * SLAB
    k2 = np.arange(1, SLAB)[:, None]
    t2 = np.arange(SLAB_PAD)[None, :]
    ang = 2.0 * np.pi * ((k2 * t2) % SLAB) / SLAB
    valid = (t2 < SLAB)
    fm = np.concatenate([np.cos(ang) * valid, np.sin(ang) * valid], axis=1)
    k1 = np.arange(N_SLAB)[:, None]
    ang2 = 2.0 * np.pi * ((k1 * t2) % n) / n
    twc = (np.cos(ang2) * valid)[:, :, None]
    tws = (np.sin(ang2) * valid)[:, :, None]
    return (jnp.asarray(fm, f32), jnp.asarray(twc, f32), jnp.asarray(tws, f32))


def _mix_out_kernel(h1_ref, hs_ref, gate_ref, four_ref, gl_ref, fb_ref, gf_ref, wo_ref,
                    g2_ref, w1_ref, w2_ref, gfin_ref, o_ref, mix_ref, act_ref):
    y_l = _rms_norm(hs_ref[...] * _gelu_tanh(gate_ref[...]), gl_ref[...])
    y_f = _rms_norm(four_ref[...] + fb_ref[...], gf_ref[...])
    mix_ref[:, 0:LRU_WIDTH] = y_l.astype(bf16)
    mix_ref[:, LRU_WIDTH:] = y_f.astype(bf16)
    h2 = h1_ref[...] + jnp.dot(mix_ref[...], wo_ref[...], preferred_element_type=f32)
    xn = _rms_norm(h2, g2_ref[...]).astype(bf16)
    h3 = h2 + 0.5 * _swiglu(xn, w1_ref, w2_ref, act_ref)
    o_ref[...] = _rms_norm(h3, gfin_ref[...])


def _mix_out(h1, hs, gate, four, gl, fb, gf, wo, g2, w1, w2, gfin, tile):
    rows = h1.shape[0]
    row_spec = lambda w: pl.BlockSpec((tile, w), lambda i: (i, 0))
    return pl.pallas_call(
        _mix_out_kernel,
        grid=(rows // tile,),
        in_specs=[row_spec(D_MODEL), row_spec(LRU_WIDTH), row_spec(LRU_WIDTH), row_spec(FOURIER_WIDTH),
                  _resident((1, LRU_WIDTH)), _resident((1, FOURIER_WIDTH)), _resident((1, FOURIER_WIDTH)),
                  _resident(wo.shape), _resident((1, D_MODEL)), _resident(w1.shape), _resident(w2.shape),
                  _resident((1, D_MODEL))],
        out_specs=row_spec(D_MODEL),
        out_shape=jax.ShapeDtypeStruct((rows, D_MODEL), f32),
        scratch_shapes=[pltpu.VMEM((tile, D_MODEL), bf16), pltpu.VMEM((tile, D_FF), bf16)],
        compiler_params=pltpu.CompilerParams(dimension_semantics=("arbitrary",), vmem_limit_bytes=VMEM_LIMIT),
        name="mix_out",
    )(h1, hs, gate, four, gl, fb, gf, wo, g2, w1, w2, gfin)


def _gate_weights(wa_f, wx_f, wa_b, wx_b, ba_f, bx_f, ba_b, bx_b):
    n_slab = LRU_WIDTH // LANES
    per = LANES // LRU_HEAD_DIM
    eye = jnp.eye(per, dtype=f32)

    def slab_bd(w):
        w = w.astype(f32).reshape(n_slab, per, LRU_HEAD_DIM, LRU_HEAD_DIM)
        return (eye[None, :, None, :, None] * w[:, :, :, None, :]).reshape(n_slab, LANES, LANES)

    wg = jnp.concatenate([slab_bd(w) for w in (wa_f, wx_f, wa_b, wx_b)], axis=-1)
    bg = jnp.concatenate([b.astype(f32).reshape(n_slab, 1, LANES) for b in (ba_f, bx_f, ba_b, bx_b)], axis=-1)
    return (0.5 * wg).astype(bf16), 0.5 * bg


def kernel(x, meta_tokens, norm_ffn1, w_ffn1_in, w_ffn1_out, norm_mix, w_in, conv_w, conv_b, lru_wa_fwd, lru_ba_fwd, lru_wx_fwd, lru_bx_fwd, lru_lambda_fwd, lru_wa_bwd, lru_ba_bwd, lru_wx_bwd, lru_bx_bwd, lru_lambda_bwd, fourier_w, fourier_b, norm_lru_out, norm_fourier_out, w_out, norm_ffn2, w_ffn2_in, w_ffn2_out, norm_final):
    batch, seq, d = x.shape
    assert (d, seq + N_META) == (D_MODEL, N_SLAB * SLAB) and norm_ffn1.shape[0] == 1
    l = 0
    row = lambda v: v.astype(f32).reshape(1, -1)

    w_four = _fold_fourier_weights(w_in[l][:, 2 * LRU_WIDTH:], fourier_w[l])
    win = jnp.concatenate([w_in[l][:, :2 * LRU_WIDTH].astype(f32), w_four], axis=1).astype(bf16)
    w1a, w2a = w_ffn1_in[l].astype(bf16), w_ffn1_out[l].astype(bf16)
    w1b, w2b = w_ffn2_in[l].astype(bf16), w_ffn2_out[l].astype(bf16)
    wo = w_out[l].astype(bf16)
    wg, bg = _gate_weights(lru_wa_fwd[l], lru_wx_fwd[l], lru_wa_bwd[l], lru_wx_bwd[l],
                           lru_ba_fwd[l], lru_bx_fwd[l], lru_ba_bwd[l], lru_bx_bwd[l])
    lam = jnp.stack([lru_lambda_fwd[l], lru_lambda_bwd[l]]).astype(f32)

    ffn_in = functools.partial(_ffn_in, g1=row(norm_ffn1[l]), w1=w1a, w2=w2a, g2=row(norm_mix[l]), win=win)
    h1, lru_x, gate, ab = ffn_in(x.reshape(batch * seq, d).astype(f32), tile=ROW_TILE)
    _, lru_m, _, ab_m = ffn_in(meta_tokens.astype(f32), tile=N_META)

    hsum = _lru(lru_x.reshape(batch, seq, LRU_WIDTH), lru_m, conv_w[l].astype(f32), row(conv_b[l]),
                wg, bg, lam, batch, seq)

    fm, twc, tws = _fourier_constants()
    four = _fourier(ab.reshape(batch, seq, 2 * FOURIER_WIDTH), ab_m, fm.astype(bf16),
                    jnp.broadcast_to(twc, (N_SLAB, SLAB_PAD, LANES)),
                    jnp.broadcast_to(tws, (N_SLAB, SLAB_PAD, LANES)), batch, seq)

    out = _mix_out(h1, hsum.reshape(batch * seq, LRU_WIDTH), gate, four.reshape(batch * seq, FOURIER_WIDTH),
                   row(norm_lru_out[l]), row(fourier_b[l]), row(norm_fourier_out[l]), wo,
                   row(norm_ffn2[l]), w1b, w2b, row(norm_final), tile=ROW_TILE)
    return out.reshape(batch, seq, d).astype(x.dtype)
```

```python
import functools
import math

import numpy as np
import jax
import jax.numpy as jnp
from jax import lax
from jax.experimental import pallas as pl
from jax.experimental.pallas import tpu as pltpu

D_MODEL = 1024
N_META = 16
LRU_WIDTH = 512
LRU_HEADS = 8
LRU_HEAD_DIM = 64
FOURIER_WIDTH = 512
FOURIER_GROUPS = 8
FOURIER_GROUP_DIM = 64
D_FF = 2816
EPS = 1e-6
RG_LRU_C = 8.0

LANES = 128
SUBLANES = 8
ROW_TILE = 512
MXU_COLS = 256
FF_CHUNKS = (5 * MXU_COLS, 6 * MXU_COLS)
VMEM_LIMIT = 60 * 1024 * 1024

N_SLAB = N_META
SLAB = 513
SLAB_PAD = 520
N_SEG = 16
SEG_LEN = 513
LRU_CHUNK = 456
SCAN_UNROLL = 9

f32 = jnp.float32
bf16 = jnp.bfloat16


def _rms_norm(x, g):
    return x * lax.rsqrt(jnp.mean(x * x, axis=-1, keepdims=True) + EPS) * g


def _sigmoid(x):
    return 0.5 * (1.0 + jnp.tanh(0.5 * x))


def _gelu_tanh(x):
    c = math.sqrt(2.0 / math.pi)
    return 0.5 * x * (1.0 + jnp.tanh(c * (x + 0.044715 * (x * x * x))))


def _swiglu(xn, w1_ref, w2_ref, act_ref):
    assert sum(FF_CHUNKS) == D_FF
    lo = 0
    for width in FF_CHUNKS:
        g = jnp.dot(xn, w1_ref[:, lo:lo + width], preferred_element_type=f32)
        u = jnp.dot(xn, w1_ref[:, D_FF + lo:D_FF + lo + width], preferred_element_type=f32)
        act_ref[:, lo:lo + width] = (g * _sigmoid(g) * u).astype(bf16)
        lo += width
    return jnp.dot(act_ref[...], w2_ref[...], preferred_element_type=f32)


def _resident(shape):
    nd = len(shape)
    return pl.BlockSpec(shape, lambda *_: (0,) * nd, pipeline_mode=pl.Buffered(1))


def _chanmix_kernel(bd_ref, cd_ref, sd_ref, o_ref):
    hi = lax.Precision.HIGHEST
    for h in range(FOURIER_WIDTH // MXU_COLS):
        blk = slice(h * MXU_COLS, (h + 1) * MXU_COLS)
        bd = bd_ref[blk, blk]
        o_ref[h, :, 0:MXU_COLS] = jnp.dot(cd_ref[blk, blk], bd, preferred_element_type=f32, precision=hi)
        o_ref[h, :, MXU_COLS:] = jnp.dot(sd_ref[blk, blk], bd, preferred_element_type=f32, precision=hi)


def _fourier_channel_mix(fourier_w):
    g, n = FOURIER_GROUPS, FOURIER_GROUP_DIM
    eye = np.eye(g, dtype=np.float64)
    ang = 2.0 * np.pi * (np.outer(np.arange(n), np.arange(n)) % n) / n
    scale = 1.0 / math.sqrt((N_SLAB * SLAB) * n)
    cd = jnp.asarray(np.kron(eye, np.cos(ang)) * scale, f32)
    sd = jnp.asarray(np.kron(eye, -np.sin(ang)) * scale, f32)
    bd = (jnp.eye(g, dtype=f32)[:, None, :, None] * fourier_w.astype(f32)[:, :, None, :]).reshape(g * n, g * n)
    return pl.pallas_call(
        _chanmix_kernel,
        out_shape=jax.ShapeDtypeStruct((FOURIER_WIDTH // MXU_COLS, MXU_COLS, 2 * MXU_COLS), f32),
        name="fourier_channel_mix",
    )(bd, cd, sd)


def _ffn_in_kernel(x_ref, g1_ref, w1_ref, w2_ref, g2_ref, win_ref, mix_ref,
                   h1_ref, lru_ref, gate_ref, ab_ref, act_ref):
    x = x_ref[...]
    xn = _rms_norm(x, g1_ref[...]).astype(bf16)
    h1 = x + 0.5 * _swiglu(xn, w1_ref, w2_ref, act_ref)
    h1_ref[...] = h1
    xn2 = _rms_norm(h1, g2_ref[...]).astype(bf16)
    u = jnp.dot(xn2, win_ref[...], preferred_element_type=f32)
    lru_ref[...] = u[:, 0:LRU_WIDTH]
    gate_ref[...] = u[:, LRU_WIDTH:2 * LRU_WIDTH]
    for h in range(FOURIER_WIDTH // MXU_COLS):
        v = u[:, 2 * LRU_WIDTH + h * MXU_COLS:2 * LRU_WIDTH + (h + 1) * MXU_COLS].astype(bf16)
        ab = jnp.dot(v, mix_ref[h], preferred_element_type=f32)
        ab_ref[:, h * MXU_COLS:(h + 1) * MXU_COLS] = ab[:, 0:MXU_COLS]
        ab_ref[:, FOURIER_WIDTH + h * MXU_COLS:FOURIER_WIDTH + (h + 1) * MXU_COLS] = ab[:, MXU_COLS:]


def _ffn_in(x2d, g1, w1, w2, g2, win, mix, tile):
    rows = x2d.shape[0]
    row_spec = lambda w: pl.BlockSpec((tile, w), lambda i: (i, 0))
    return pl.pallas_call(
        _ffn_in_kernel,
        grid=(rows // tile,),
        in_specs=[row_spec(D_MODEL), _resident((1, D_MODEL)), _resident(w1.shape), _resident(w2.shape),
                  _resident((1, D_MODEL)), _resident(win.shape), _resident(mix.shape)],
        out_specs=[row_spec(D_MODEL), row_spec(LRU_WIDTH), row_spec(LRU_WIDTH), row_spec(2 * FOURIER_WIDTH)],
        out_shape=[jax.ShapeDtypeStruct((rows, D_MODEL), f32),
                   jax.ShapeDtypeStruct((rows, LRU_WIDTH), f32),
                   jax.ShapeDtypeStruct((rows, LRU_WIDTH), f32),
                   jax.ShapeDtypeStruct((rows, 2 * FOURIER_WIDTH), f32)],
        scratch_shapes=[pltpu.VMEM((tile, D_FF), bf16)],
        compiler_params=pltpu.CompilerParams(dimension_semantics=("arbitrary",), vmem_limit_bytes=VMEM_LIMIT),
        name="ffn_in",
    )(x2d, g1, w1, w2, g2, win, mix)


def _lru_kernel(xl_ref, xm_ref, cw_ref, cb_ref, wg_ref, bg_ref, lam_ref, o_ref,
                seq_ref, af_ref, bf_ref, ab_ref, bb_ref, hf_ref, hb_ref, ends_ref, carry_ref):
    t_all = N_SEG * SEG_LEN
    zero8 = jnp.zeros((SUBLANES, LANES), f32)
    seq_ref[0:SUBLANES, :] = zero8
    seq_ref[SUBLANES:SUBLANES + N_META, :] = xm_ref[...]
    seq_ref[SUBLANES + N_META:SUBLANES + t_all, :] = xl_ref[...]
    seq_ref[SUBLANES + t_all:2 * SUBLANES + t_all, :] = zero8

    cw = cw_ref[...]
    cb = cb_ref[...]
    bg = bg_ref[...]
    lam = lam_ref[...]
    nlam = -lam
    sp = jnp.maximum(nlam, 0.0) + jnp.log1p(jnp.exp(-jnp.abs(nlam)))
    k_f = (-0.25 * RG_LRU_C) * sp[0:1, :]
    k_b = (-0.25 * RG_LRU_C) * sp[1:2, :]

    def gates(half_pre_r, half_pre_i, xc, k_dir):
        t = jnp.tanh(k_dir * jnp.tanh(half_pre_r) + k_dir)
        q = 1.0 / (1.0 - t)
        a = (1.0 + t) * q
        return a, (jnp.sqrt(-t) * q) * ((1.0 + jnp.tanh(half_pre_i)) * xc)

    def chunk_body(c, carry):
        r0 = pl.multiple_of(c * LRU_CHUNK, SUBLANES)
        xc = (cw[0:1, :] * seq_ref[pl.ds(r0 + 6, LRU_CHUNK), :] + cw[1:2, :] * seq_ref[pl.ds(r0 + 7, LRU_CHUNK), :]
              + cw[2:3, :] * seq_ref[pl.ds(r0 + 8, LRU_CHUNK), :] + cw[3:4, :] * seq_ref[pl.ds(r0 + 9, LRU_CHUNK), :]
              + cb)
        pre = jnp.dot(xc.astype(bf16), wg_ref[...], preferred_element_type=f32) + bg
        a_f, b_f = gates(pre[:, 0:LANES], pre[:, LANES:2 * LANES], xc, k_f)
        a_b, b_b = gates(pre[:, 2 * LANES:3 * LANES], pre[:, 3 * LANES:4 * LANES], xc, k_b)
        rows = pl.ds(r0, LRU_CHUNK)
        af_ref[rows, :] = a_f
        bf_ref[rows, :] = b_f
        ab_ref[rows, :] = a_b
        bb_ref[rows, :] = b_b
        return carry

    lax.fori_loop(0, t_all // LRU_CHUNK, chunk_body, 0)

    half = SUBLANES * SEG_LEN

    def seg_rows(j, hi):
        return pl.ds(j + hi * half, SUBLANES, stride=SEG_LEN)

    def pass1(j, st):
        hf0, hf1, pf0, pf1, hb0, hb1, pb0, pb1 = st
        jb = SEG_LEN - 1 - j
        a0 = af_ref[seg_rows(j, 0), :]
        a1 = af_ref[seg_rows(j, 1), :]
        hf0 = a0 * hf0 + bf_ref[seg_rows(j, 0), :]
        hf1 = a1 * hf1 + bf_ref[seg_rows(j, 1), :]
        pf0 = a0 * pf0
        pf1 = a1 * pf1
        c0 = ab_ref[seg_rows(jb, 0), :]
        c1 = ab_ref[seg_rows(jb, 1), :]
        hb0 = c0 * hb0 + bb_ref[seg_rows(jb, 0), :]
        hb1 = c1 * hb1 + bb_ref[seg_rows(jb, 1), :]
        pb0 = c0 * pb0
        pb1 = c1 * pb1
        return hf0, hf1, pf0, pf1, hb0, hb1, pb0, pb1

    one8 = jnp.ones((SUBLANES, LANES), f32)
    st = lax.fori_loop(0, SEG_LEN, pass1, (zero8, zero8, one8, one8, zero8, zero8, one8, one8),
                       unroll=SCAN_UNROLL)
    for idx, v in enumerate(st):
        ends_ref[idx * SUBLANES:(idx + 1) * SUBLANES, :] = v

    c = jnp.zeros((1, LANES), f32)
    for s in range(N_SEG):
        carry_ref[s:s + 1, :] = c
        c = ends_ref[s:s + 1, :] + ends_ref[N_SEG + s:N_SEG + s + 1, :] * c
    c = jnp.zeros((1, LANES), f32)
    for s in range(N_SEG - 1, -1, -1):
        carry_ref[N_SEG + s:N_SEG + s + 1, :] = c
        c = ends_ref[2 * N_SEG + s:2 * N_SEG + s + 1, :] + ends_ref[3 * N_SEG + s:3 * N_SEG + s + 1, :] * c

    def pass2(j, st):
        hf0, hf1, hb0, hb1 = st
        jb = SEG_LEN - 1 - j
        hf0 = af_ref[seg_rows(j, 0), :] * hf0 + bf_ref[seg_rows(j, 0), :]
        hf1 = af_ref[seg_rows(j, 1), :] * hf1 + bf_ref[seg_rows(j, 1), :]
        hf_ref[seg_rows(j, 0), :] = hf0
        hf_ref[seg_rows(j, 1), :] = hf1
        hb0 = ab_ref[seg_rows(jb, 0), :] * hb0 + bb_ref[seg_rows(jb, 0), :]
        hb1 = ab_ref[seg_rows(jb, 1), :] * hb1 + bb_ref[seg_rows(jb, 1), :]
        hb_ref[seg_rows(jb, 0), :] = hb0
        hb_ref[seg_rows(jb, 1), :] = hb1
        return hf0, hf1, hb0, hb1

    lax.fori_loop(0, SEG_LEN, pass2,
                  (carry_ref[0:SUBLANES, :], carry_ref[SUBLANES:2 * SUBLANES, :],
                   carry_ref[2 * SUBLANES:3 * SUBLANES, :], carry_ref[3 * SUBLANES:4 * SUBLANES, :]),
                  unroll=SCAN_UNROLL)

    o_ref[...] = hf_ref[N_META:t_all, :] + hb_ref[N_META:t_all, :]


def _lru(lru_x, lru_meta, conv_w, conv_b, wg, bg, lam, batch, seq):
    t_all = N_META + seq
    n_slab = LRU_WIDTH // LANES
    slab = lambda r: pl.BlockSpec((r, LANES), lambda b, s: (0, s))
    return pl.pallas_call(
        _lru_kernel,
        grid=(batch, n_slab),
        in_specs=[pl.BlockSpec((None, seq, LANES), lambda b, s: (b, 0, s)),
                  slab(N_META), slab(4), slab(1),
                  pl.BlockSpec((None, LANES, 4 * LANES), lambda b, s: (s, 0, 0)),
                  pl.BlockSpec((None, 1, 4 * LANES), lambda b, s: (s, 0, 0)),
                  slab(2)],
        out_specs=pl.BlockSpec((None, seq, LANES), lambda b, s: (b, 0, s)),
        out_shape=jax.ShapeDtypeStruct((batch, seq, LRU_WIDTH), f32),
        scratch_shapes=[pltpu.VMEM((t_all + 2 * SUBLANES, LANES), f32)]
                       + [pltpu.VMEM((t_all, LANES), f32)] * 6
                       + [pltpu.VMEM((4 * N_SEG, LANES), f32), pltpu.VMEM((2 * N_SEG, LANES), f32)],
        compiler_params=pltpu.CompilerParams(dimension_semantics=("arbitrary", "arbitrary"),
                                             vmem_limit_bytes=VMEM_LIMIT),
        name="lru",
    )(lru_x, lru_meta, conv_w, conv_b, wg, bg, lam)


def _cmul_const(x, c, s):
    re, im = x
    tol = 1e-12
    if abs(s) < tol:
        return (re, im) if c > 0 else (-re, -im)
    if abs(c) < tol:
        return (-im, re) if s > 0 else (im, -re)
    if abs(abs(c) - abs(s)) < tol:
        m = abs(c)
        sc, ss = (1.0 if c > 0 else -1.0), (1.0 if s > 0 else -1.0)
        return (m * (sc * re - ss * im), m * (ss * re + sc * im))
    return (c * re - s * im, s * re + c * im)


def _fft(xs):
    n = len(xs)
    if n == 1:
        return xs
    ev = _fft(xs[0::2])
    od = _fft(xs[1::2])
    out = [None] * n
    for k in range(n // 2):
        ang = -2.0 * math.pi * k / n
        t = _cmul_const(od[k], math.cos(ang), math.sin(ang))
        out[k] = (ev[k][0] + t[0], ev[k][1] + t[1])
        out[k + n // 2] = (ev[k][0] - t[0], ev[k][1] - t[1])
    return out


def _fourier_kernel(xa_ref, xb_ref, ma_ref, mb_ref, fm_ref, twc_ref, tws_ref, o_ref,
                    ha_ref, hb_ref, ta_ref, tb_ref, y_ref):
    tail0 = (N_SLAB - 1) * SLAB - N_META
    for m_ref, x_ref, h_ref, t_ref in ((ma_ref, xa_ref, ha_ref, ta_ref), (mb_ref, xb_ref, hb_ref, tb_ref)):
        h_ref[0:N_META, :] = m_ref[...]
        h_ref[N_META:SLAB_PAD, :] = x_ref[0:SLAB_PAD - N_META, :]
        t_ref[SLAB - 1:SLAB_PAD, :] = jnp.zeros((SLAB_PAD - SLAB + 1, LANES), f32)
        t_ref[0:SLAB - 1, :] = x_ref[tail0:tail0 + SLAB - 1, :]
        t_ref[SLAB - 1:SLAB, :] = x_ref[tail0 + SLAB - 1:tail0 + SLAB, :]

    def stage1(it, carry):
        r = pl.multiple_of(it * SUBLANES, SUBLANES)
        rows = pl.ds(r, SUBLANES)

        def slab_rows(x_ref, h_ref, t_ref, t1):
            if t1 == 0:
                return h_ref[rows, :]
            if t1 == N_SLAB - 1:
                return t_ref[rows, :]
            return x_ref[pl.ds(r + (t1 * SLAB - N_META), SUBLANES), :]

        zs = [(slab_rows(xa_ref, ha_ref, ta_ref, t1), slab_rows(xb_ref, hb_ref, tb_ref, t1))
              for t1 in range(N_SLAB)]
        ys = _fft(zs)
        for k1 in range(N_SLAB):
            yr, yi = ys[k1]
            if k1 > 0:
                cw = twc_ref[k1, rows, :]
                sw = tws_ref[k1, rows, :]
                yr, yi = yr * cw + yi * sw, yi * cw - yr * sw
            y_ref[rows, k1 * LANES:(k1 + 1) * LANES] = yr
            y_ref[pl.ds(r + SLAB_PAD, SUBLANES), k1 * LANES:(k1 + 1) * LANES] = yi
        return carry

    lax.fori_loop(0, SLAB_PAD // SUBLANES, stage1, 0)

    for p in range(N_SLAB // 2):
        yp = y_ref[:, 2 * p * LANES:(2 * p + 2) * LANES].astype(bf16)
        g = jnp.dot(fm_ref[...], yp, preferred_element_type=f32)
        o_ref[pl.ds(2 * p, SLAB - 1, stride=N_SLAB), :] = g[:, 0:LANES]
        o_ref[pl.ds(2 * p + 1, SLAB - 1, stride=N_SLAB), :] = g[:, LANES:2 * LANES]


def _fourier(ab, ab_m, fm, twc, tws, batch, seq):
    n_blk = FOURIER_WIDTH // LANES
    x_spec = lambda off: pl.BlockSpec((None, seq, LANES), lambda b, j: (b, 0, j + off))
    m_spec = lambda off: pl.BlockSpec((N_META, LANES), lambda b, j: (0, j + off))
    return pl.pallas_call(
        _fourier_kernel,
        grid=(batch, n_blk),
        in_specs=[x_spec(0), x_spec(n_blk), m_spec(0), m_spec(n_blk),
                  _resident(fm.shape), _resident(twc.shape), _resident(tws.shape)],
        out_specs=pl.BlockSpec((None, seq, LANES), lambda b, j: (b, 0, j)),
        out_shape=jax.ShapeDtypeStruct((batch, seq, FOURIER_WIDTH), f32),
        scratch_shapes=[pltpu.VMEM((SLAB_PAD, LANES), f32)] * 4
                       + [pltpu.VMEM((2 * SLAB_PAD, N_SLAB * LANES), f32)],
        compiler_params=pltpu.CompilerParams(dimension_semantics=("arbitrary", "arbitrary"),
                                             vmem_limit_bytes=VMEM_LIMIT),
        name="fourier",
    )(ab, ab, ab_m, ab_m, fm, twc, tws)


def _fourier_constants():
    n = N_SLAB * SLAB
    k2 = np.arange(1, SLAB)[:, None]
    t2 = np.arange(SLAB_PAD)[None, :]
    ang = 2.0 * np.pi * ((k2 * t2) % SLAB) / SLAB
    valid = (t2 < SLAB)
    fm = np.concatenate([np.cos(ang) * valid, np.sin(ang) * valid], axis=1)
    k1 = np.arange(N_SLAB)[:, None]
    ang2 = 2.0 * np.pi * ((k1 * t2) % n) / n
    lane_bcast = np.ones((1, 1, LANES))
    twc = (np.cos(ang2) * valid)[:, :, None] * lane_bcast
    tws = (np.sin(ang2) * valid)[:, :, None] * lane_bcast
    return (jnp.asarray(fm, f32), jnp.asarray(twc, f32), jnp.asarray(tws, f32))


def _mix_out_kernel(h1_ref, hs_ref, gate_ref, four_ref, gl_ref, fb_ref, gf_ref, wo_ref,
                    g2_ref, w1_ref, w2_ref, gfin_ref, o_ref, mix_ref, act_ref):
    y_l = _rms_norm(hs_ref[...] * _gelu_tanh(gate_ref[...]), gl_ref[...])
    y_f = _rms_norm(four_ref[...] + fb_ref[...], gf_ref[...])
    mix_ref[:, 0:LRU_WIDTH] = y_l.astype(bf16)
    mix_ref[:, LRU_WIDTH:] = y_f.astype(bf16)
    h2 = h1_ref[...] + jnp.dot(mix_ref[...], wo_ref[...], preferred_element_type=f32)
    xn = _rms_norm(h2, g2_ref[...]).astype(bf16)
    h3 = h2 + 0.5 * _swiglu(xn, w1_ref, w2_ref, act_ref)
    o_ref[...] = _rms_norm(h3, gfin_ref[...])


def _mix_out(h1, hs, gate, four, gl, fb, gf, wo, g2, w1, w2, gfin, tile):
    rows = h1.shape[0]
    row_spec = lambda w: pl.BlockSpec((tile, w), lambda i: (i, 0))
    return pl.pallas_call(
        _mix_out_kernel,
        grid=(rows // tile,),
        in_specs=[row_spec(D_MODEL), row_spec(LRU_WIDTH), row_spec(LRU_WIDTH), row_spec(FOURIER_WIDTH),
                  _resident((1, LRU_WIDTH)), _resident((1, FOURIER_WIDTH)), _resident((1, FOURIER_WIDTH)),
                  _resident(wo.shape), _resident((1, D_MODEL)), _resident(w1.shape), _resident(w2.shape),
                  _resident((1, D_MODEL))],
        out_specs=row_spec(D_MODEL),
        out_shape=jax.ShapeDtypeStruct((rows, D_MODEL), f32),
        scratch_shapes=[pltpu.VMEM((tile, D_MODEL), bf16), pltpu.VMEM((tile, D_FF), bf16)],
        compiler_params=pltpu.CompilerParams(dimension_semantics=("arbitrary",), vmem_limit_bytes=VMEM_LIMIT),
        name="mix_out",
    )(h1, hs, gate, four, gl, fb, gf, wo, g2, w1, w2, gfin)


def _gate_weights(wa_f, wx_f, wa_b, wx_b, ba_f, bx_f, ba_b, bx_b):
    n_slab = LRU_WIDTH // LANES
    per = LANES // LRU_HEAD_DIM
    eye = jnp.eye(per, dtype=f32)

    def slab_bd(w):
        w = w.astype(f32).reshape(n_slab, per, LRU_HEAD_DIM, LRU_HEAD_DIM)
        return (eye[None, :, None, :, None] * w[:, :, :, None, :]).reshape(n_slab, LANES, LANES)

    wg = jnp.concatenate([slab_bd(w) for w in (wa_f, wx_f, wa_b, wx_b)], axis=-1)
    bg = jnp.concatenate([b.astype(f32).reshape(n_slab, 1, LANES) for b in (ba_f, bx_f, ba_b, bx_b)], axis=-1)
    return (0.5 * wg).astype(bf16), 0.5 * bg


def kernel(x, meta_tokens, norm_ffn1, w_ffn1_in, w_ffn1_out, norm_mix, w_in, conv_w, conv_b, lru_wa_fwd, lru_ba_fwd, lru_wx_fwd, lru_bx_fwd, lru_lambda_fwd, lru_wa_bwd, lru_ba_bwd, lru_wx_bwd, lru_bx_bwd, lru_lambda_bwd, fourier_w, fourier_b, norm_lru_out, norm_fourier_out, w_out, norm_ffn2, w_ffn2_in, w_ffn2_out, norm_final):
    batch, seq, d = x.shape
    assert (d, seq + N_META) == (D_MODEL, N_SLAB * SLAB) and norm_ffn1.shape[0] == 1
    l = 0
    row = lambda v: v.astype(f32).reshape(1, -1)

    mix = _fourier_channel_mix(fourier_w[l]).astype(bf16)
    win = w_in[l].astype(bf16)
    w1a, w2a = w_ffn1_in[l].astype(bf16), w_ffn1_out[l].astype(bf16)
    w1b, w2b = w_ffn2_in[l].astype(bf16), w_ffn2_out[l].astype(bf16)
    wo = w_out[l].astype(bf16)
    wg, bg = _gate_weights(lru_wa_fwd[l], lru_wx_fwd[l], lru_wa_bwd[l], lru_wx_bwd[l],
                           lru_ba_fwd[l], lru_bx_fwd[l], lru_ba_bwd[l], lru_bx_bwd[l])
    lam = jnp.stack([lru_lambda_fwd[l], lru_lambda_bwd[l]]).astype(f32)

    ffn_in = functools.partial(_ffn_in, g1=row(norm_ffn1[l]), w1=w1a, w2=w2a, g2=row(norm_mix[l]), win=win, mix=mix)
    h1, lru_x, gate, ab = ffn_in(x.reshape(batch * seq, d).astype(f32), tile=ROW_TILE)
    _, lru_m, _, ab_m = ffn_in(meta_tokens.astype(f32), tile=N_META)

    hsum = _lru(lru_x.reshape(batch, seq, LRU_WIDTH), lru_m, conv_w[l].astype(f32), row(conv_b[l]),
                wg, bg, lam, batch, seq)

    fm, twc, tws = _fourier_constants()
    four = _fourier(ab.reshape(batch, seq, 2 * FOURIER_WIDTH), ab_m, fm.astype(bf16), twc, tws, batch, seq)

    out = _mix_out(h1, hsum.reshape(batch * seq, LRU_WIDTH), gate, four.reshape(batch * seq, FOURIER_WIDTH),
                   row(norm_lru_out[l]), row(fourier_b[l]), row(norm_fourier_out[l]), wo,
                   row(norm_ffn2[l]), w1b, w2b, row(norm_final), tile=ROW_TILE)
    return out.reshape(batch, seq, d).astype(x.dtype)
```

```python
import functools
import math

import numpy as np
import jax
import jax.numpy as jnp
from jax import lax
from jax.experimental import pallas as pl
from jax.experimental.pallas import tpu as pltpu

D_MODEL = 1024
N_META = 16
LRU_WIDTH = 512
LRU_HEADS = 8
LRU_HEAD_DIM = 64
FOURIER_WIDTH = 512
FOURIER_GROUPS = 8
FOURIER_GROUP_DIM = 64
D_FF = 2816
EPS = 1e-6
RG_LRU_C = 8.0

LANES = 128
SUBLANES = 8
ROW_TILE = 512
MXU_COLS = 256
FF_CHUNKS = (MXU_COLS,) * 11
VMEM_LIMIT = 60 * 1024 * 1024

N_SLAB = N_META
SLAB = 513
SLAB_PAD = 520
N_SEG = 16
SEG_LEN = 513
LRU_CHUNK = 456
SCAN_UNROLL = 9

f32 = jnp.float32
bf16 = jnp.bfloat16


def _rms_norm(x, g):
    return x * lax.rsqrt(jnp.mean(x * x, axis=-1, keepdims=True) + EPS) * g


def _sigmoid(x):
    return 0.5 * (1.0 + jnp.tanh(0.5 * x))


def _gelu_tanh(x):
    c = math.sqrt(2.0 / math.pi)
    return 0.5 * x * (1.0 + jnp.tanh(c * (x + 0.044715 * (x * x * x))))


def _swiglu(xn, w1_ref, w2_ref, act_ref):
    assert sum(FF_CHUNKS) == D_FF
    lo = 0
    for width in FF_CHUNKS:
        g = jnp.dot(xn, w1_ref[:, lo:lo + width], preferred_element_type=f32)
        u = jnp.dot(xn, w1_ref[:, D_FF + lo:D_FF + lo + width], preferred_element_type=f32)
        act_ref[:, lo:lo + width] = (g * _sigmoid(g) * u).astype(bf16)
        lo += width
    return jnp.dot(act_ref[...], w2_ref[...], preferred_element_type=f32)


def _resident(shape):
    nd = len(shape)
    return pl.BlockSpec(shape, lambda *_: (0,) * nd, pipeline_mode=pl.Buffered(1))


def _chanmix_kernel(bd_ref, cd_ref, sd_ref, o_ref):
    hi = lax.Precision.HIGHEST
    for h in range(FOURIER_WIDTH // MXU_COLS):
        blk = slice(h * MXU_COLS, (h + 1) * MXU_COLS)
        bd = bd_ref[blk, blk]
        o_ref[h, :, 0:MXU_COLS] = jnp.dot(cd_ref[blk, blk], bd, preferred_element_type=f32, precision=hi)
        o_ref[h, :, MXU_COLS:] = jnp.dot(sd_ref[blk, blk], bd, preferred_element_type=f32, precision=hi)


def _fourier_channel_mix(fourier_w):
    g, n = FOURIER_GROUPS, FOURIER_GROUP_DIM
    eye = np.eye(g, dtype=np.float64)
    ang = 2.0 * np.pi * (np.outer(np.arange(n), np.arange(n)) % n) / n
    scale = 1.0 / math.sqrt((N_SLAB * SLAB) * n)
    cd = jnp.asarray(np.kron(eye, np.cos(ang)) * scale, f32)
    sd = jnp.asarray(np.kron(eye, -np.sin(ang)) * scale, f32)
    bd = (jnp.eye(g, dtype=f32)[:, None, :, None] * fourier_w.astype(f32)[:, :, None, :]).reshape(g * n, g * n)
    return pl.pallas_call(
        _chanmix_kernel,
        out_shape=jax.ShapeDtypeStruct((FOURIER_WIDTH // MXU_COLS, MXU_COLS, 2 * MXU_COLS), f32),
        name="fourier_channel_mix",
    )(bd, cd, sd)


def _ffn_in_kernel(x_ref, g1_ref, w1_ref, w2_ref, g2_ref, win_ref, mix_ref,
                   h1_ref, lru_ref, gate_ref, ab_ref, act_ref):
    x = x_ref[...]
    xn = _rms_norm(x, g1_ref[...]).astype(bf16)
    h1 = x + 0.5 * _swiglu(xn, w1_ref, w2_ref, act_ref)
    h1_ref[...] = h1
    xn2 = _rms_norm(h1, g2_ref[...]).astype(bf16)
    u = jnp.dot(xn2, win_ref[...], preferred_element_type=f32)
    lru_ref[...] = u[:, 0:LRU_WIDTH]
    gate_ref[...] = u[:, LRU_WIDTH:2 * LRU_WIDTH]
    for h in range(FOURIER_WIDTH // MXU_COLS):
        v = u[:, 2 * LRU_WIDTH + h * MXU_COLS:2 * LRU_WIDTH + (h + 1) * MXU_COLS].astype(bf16)
        ab = jnp.dot(v, mix_ref[h], preferred_element_type=f32)
        ab_ref[:, h * MXU_COLS:(h + 1) * MXU_COLS] = ab[:, 0:MXU_COLS]
        ab_ref[:, FOURIER_WIDTH + h * MXU_COLS:FOURIER_WIDTH + (h + 1) * MXU_COLS] = ab[:, MXU_COLS:]


def _ffn_in(x2d, g1, w1, w2, g2, win, mix, tile):
    rows = x2d.shape[0]
    row_spec = lambda w: pl.BlockSpec((tile, w), lambda i: (i, 0))
    return pl.pallas_call(
        _ffn_in_kernel,
        grid=(rows // tile,),
        in_specs=[row_spec(D_MODEL), _resident((1, D_MODEL)), _resident(w1.shape), _resident(w2.shape),
                  _resident((1, D_MODEL)), _resident(win.shape), _resident(mix.shape)],
        out_specs=[row_spec(D_MODEL), row_spec(LRU_WIDTH), row_spec(LRU_WIDTH), row_spec(2 * FOURIER_WIDTH)],
        out_shape=[jax.ShapeDtypeStruct((rows, D_MODEL), f32),
                   jax.ShapeDtypeStruct((rows, LRU_WIDTH), f32),
                   jax.ShapeDtypeStruct((rows, LRU_WIDTH), f32),
                   jax.ShapeDtypeStruct((rows, 2 * FOURIER_WIDTH), f32)],
        scratch_shapes=[pltpu.VMEM((tile, D_FF), bf16)],
        compiler_params=pltpu.CompilerParams(dimension_semantics=("arbitrary",), vmem_limit_bytes=VMEM_LIMIT),
        name="ffn_in",
    )(x2d, g1, w1, w2, g2, win, mix)


def _lru_kernel(xl_ref, xm_ref, cw_ref, cb_ref, wg_ref, bg_ref, lam_ref, o_ref,
                seq_ref, af_ref, bf_ref, ab_ref, bb_ref, hf_ref, hb_ref, ends_ref, carry_ref):
    t_all = N_SEG * SEG_LEN
    zero8 = jnp.zeros((SUBLANES, LANES), f32)
    seq_ref[0:SUBLANES, :] = zero8
    seq_ref[SUBLANES:SUBLANES + N_META, :] = xm_ref[...]
    seq_ref[SUBLANES + N_META:SUBLANES + t_all, :] = xl_ref[...]
    seq_ref[SUBLANES + t_all:2 * SUBLANES + t_all, :] = zero8

    cw = cw_ref[...]
    cb = cb_ref[...]
    bg = bg_ref[...]
    lam = lam_ref[...]
    nlam = -lam
    sp = jnp.maximum(nlam, 0.0) + jnp.log1p(jnp.exp(-jnp.abs(nlam)))
    k_f = (-0.25 * RG_LRU_C) * sp[0:1, :]
    k_b = (-0.25 * RG_LRU_C) * sp[1:2, :]

    def gates(half_pre_r, half_pre_i, xc, k_dir):
        t = jnp.tanh(k_dir * jnp.tanh(half_pre_r) + k_dir)
        q = 1.0 / (1.0 - t)
        a = (1.0 + t) * q
        y = -t
        root = jnp.where(y > 0.0, y * lax.rsqrt(y), 0.0)
        return a, (root * q) * ((1.0 + jnp.tanh(half_pre_i)) * xc)

    def chunk_body(c, carry):
        r0 = pl.multiple_of(c * LRU_CHUNK, SUBLANES)
        xc = (cw[0:1, :] * seq_ref[pl.ds(r0 + 6, LRU_CHUNK), :] + cw[1:2, :] * seq_ref[pl.ds(r0 + 7, LRU_CHUNK), :]
              + cw[2:3, :] * seq_ref[pl.ds(r0 + 8, LRU_CHUNK), :] + cw[3:4, :] * seq_ref[pl.ds(r0 + 9, LRU_CHUNK), :]
              + cb)
        pre = jnp.dot(xc.astype(bf16), wg_ref[...], preferred_element_type=f32) + bg
        a_f, b_f = gates(pre[:, 0:LANES], pre[:, LANES:2 * LANES], xc, k_f)
        a_b, b_b = gates(pre[:, 2 * LANES:3 * LANES], pre[:, 3 * LANES:4 * LANES], xc, k_b)
        rows = pl.ds(r0, LRU_CHUNK)
        af_ref[rows, :] = a_f
        bf_ref[rows, :] = b_f
        ab_ref[rows, :] = a_b
        bb_ref[rows, :] = b_b
        return carry

    lax.fori_loop(0, t_all // LRU_CHUNK, chunk_body, 0)

    half = SUBLANES * SEG_LEN

    def seg_rows(j, hi):
        return pl.ds(j + hi * half, SUBLANES, stride=SEG_LEN)

    def pass1(j, st):
        hf0, hf1, pf0, pf1, hb0, hb1, pb0, pb1 = st
        jb = SEG_LEN - 1 - j
        a0 = af_ref[seg_rows(j, 0), :]
        a1 = af_ref[seg_rows(j, 1), :]
        hf0 = a0 * hf0 + bf_ref[seg_rows(j, 0), :]
        hf1 = a1 * hf1 + bf_ref[seg_rows(j, 1), :]
        pf0 = a0 * pf0
        pf1 = a1 * pf1
        c0 = ab_ref[seg_rows(jb, 0), :]
        c1 = ab_ref[seg_rows(jb, 1), :]
        hb0 = c0 * hb0 + bb_ref[seg_rows(jb, 0), :]
        hb1 = c1 * hb1 + bb_ref[seg_rows(jb, 1), :]
        pb0 = c0 * pb0
        pb1 = c1 * pb1
        return hf0, hf1, pf0, pf1, hb0, hb1, pb0, pb1

    one8 = jnp.ones((SUBLANES, LANES), f32)
    st = lax.fori_loop(0, SEG_LEN, pass1, (zero8, zero8, one8, one8, zero8, zero8, one8, one8),
                       unroll=SCAN_UNROLL)
    for idx, v in enumerate(st):
        ends_ref[idx * SUBLANES:(idx + 1) * SUBLANES, :] = v

    c = jnp.zeros((1, LANES), f32)
    for s in range(N_SEG):
        carry_ref[s:s + 1, :] = c
        c = ends_ref[s:s + 1, :] + ends_ref[N_SEG + s:N_SEG + s + 1, :] * c
    c = jnp.zeros((1, LANES), f32)
    for s in range(N_SEG - 1, -1, -1):
        carry_ref[N_SEG + s:N_SEG + s + 1, :] = c
        c = ends_ref[2 * N_SEG + s:2 * N_SEG + s + 1, :] + ends_ref[3 * N_SEG + s:3 * N_SEG + s + 1, :] * c

    def pass2(j, st):
        hf0, hf1, hb0, hb1 = st
        jb = SEG_LEN - 1 - j
        hf0 = af_ref[seg_rows(j, 0), :] * hf0 + bf_ref[seg_rows(j, 0), :]
        hf1 = af_ref[seg_rows(j, 1), :] * hf1 + bf_ref[seg_rows(j, 1), :]
        hf_ref[seg_rows(j, 0), :] = hf0
        hf_ref[seg_rows(j, 1), :] = hf1
        hb0 = ab_ref[seg_rows(jb, 0), :] * hb0 + bb_ref[seg_rows(jb, 0), :]
        hb1 = ab_ref[seg_rows(jb, 1), :] * hb1 + bb_ref[seg_rows(jb, 1), :]
        hb_ref[seg_rows(jb, 0), :] = hb0
        hb_ref[seg_rows(jb, 1), :] = hb1
        return hf0, hf1, hb0, hb1

    lax.fori_loop(0, SEG_LEN, pass2,
                  (carry_ref[0:SUBLANES, :], carry_ref[SUBLANES:2 * SUBLANES, :],
                   carry_ref[2 * SUBLANES:3 * SUBLANES, :], carry_ref[3 * SUBLANES:4 * SUBLANES, :]),
                  unroll=SCAN_UNROLL)

    o_ref[...] = hf_ref[N_META:t_all, :] + hb_ref[N_META:t_all, :]


def _lru(lru_x, lru_meta, conv_w, conv_b, wg, bg, lam, batch, seq):
    t_all = N_META + seq
    n_slab = LRU_WIDTH // LANES
    slab = lambda r: pl.BlockSpec((r, LANES), lambda b, s: (0, s))
    return pl.pallas_call(
        _lru_kernel,
        grid=(batch, n_slab),
        in_specs=[pl.BlockSpec((None, seq, LANES), lambda b, s: (b, 0, s)),
                  slab(N_META), slab(4), slab(1),
                  pl.BlockSpec((None, LANES, 4 * LANES), lambda b, s: (s, 0, 0)),
                  pl.BlockSpec((None, 1, 4 * LANES), lambda b, s: (s, 0, 0)),
                  slab(2)],
        out_specs=pl.BlockSpec((None, seq, LANES), lambda b, s: (b, 0, s)),
        out_shape=jax.ShapeDtypeStruct((batch, seq, LRU_WIDTH), f32),
        scratch_shapes=[pltpu.VMEM((t_all + 2 * SUBLANES, LANES), f32)]
                       + [pltpu.VMEM((t_all, LANES), f32)] * 6
                       + [pltpu.VMEM((4 * N_SEG, LANES), f32), pltpu.VMEM((2 * N_SEG, LANES), f32)],
        compiler_params=pltpu.CompilerParams(dimension_semantics=("arbitrary", "arbitrary"),
                                             vmem_limit_bytes=VMEM_LIMIT),
        name="lru",
    )(lru_x, lru_meta, conv_w, conv_b, wg, bg, lam)


def _cmul_const(x, c, s):
    re, im = x
    tol = 1e-12
    if abs(s) < tol:
        return (re, im) if c > 0 else (-re, -im)
    if abs(c) < tol:
        return (-im, re) if s > 0 else (im, -re)
    if abs(abs(c) - abs(s)) < tol:
        m = abs(c)
        sc, ss = (1.0 if c > 0 else -1.0), (1.0 if s > 0 else -1.0)
        return (m * (sc * re - ss * im), m * (ss * re + sc * im))
    return (c * re - s * im, s * re + c * im)


def _fft(xs):
    n = len(xs)
    if n == 1:
        return xs
    ev = _fft(xs[0::2])
    od = _fft(xs[1::2])
    out = [None] * n
    for k in range(n // 2):
        ang = -2.0 * math.pi * k / n
        t = _cmul_const(od[k], math.cos(ang), math.sin(ang))
        out[k] = (ev[k][0] + t[0], ev[k][1] + t[1])
        out[k + n // 2] = (ev[k][0] - t[0], ev[k][1] - t[1])
    return out


def _fourier_kernel(xa_ref, xb_ref, ma_ref, mb_ref, fm_ref, twc_ref, tws_ref, o_ref,
                    ha_ref, hb_ref, ta_ref, tb_ref, y_ref):
    tail0 = (N_SLAB - 1) * SLAB - N_META
    for m_ref, x_ref, h_ref, t_ref in ((ma_ref, xa_ref, ha_ref, ta_ref), (mb_ref, xb_ref, hb_ref, tb_ref)):
        h_ref[0:N_META, :] = m_ref[...]
        h_ref[N_META:SLAB_PAD, :] = x_ref[0:SLAB_PAD - N_META, :]
        t_ref[SLAB - 1:SLAB_PAD, :] = jnp.zeros((SLAB_PAD - SLAB + 1, LANES), f32)
        t_ref[0:SLAB - 1, :] = x_ref[tail0:tail0 + SLAB - 1, :]
        t_ref[SLAB - 1:SLAB, :] = x_ref[tail0 + SLAB - 1:tail0 + SLAB, :]

    def stage1(it, carry):
        r = pl.multiple_of(it * SUBLANES, SUBLANES)
        rows = pl.ds(r, SUBLANES)

        def slab_rows(x_ref, h_ref, t_ref, t1):
            if t1 == 0:
                return h_ref[rows, :]
            if t1 == N_SLAB - 1:
                return t_ref[rows, :]
            return x_ref[pl.ds(r + (t1 * SLAB - N_META), SUBLANES), :]

        zs = [(slab_rows(xa_ref, ha_ref, ta_ref, t1), slab_rows(xb_ref, hb_ref, tb_ref, t1))
              for t1 in range(N_SLAB)]
        ys = _fft(zs)
        for k1 in range(N_SLAB):
            yr, yi = ys[k1]
            if k1 > 0:
                cw = twc_ref[k1, rows, :]
                sw = tws_ref[k1, rows, :]
                yr, yi = yr * cw + yi * sw, yi * cw - yr * sw
            y_ref[rows, k1 * LANES:(k1 + 1) * LANES] = yr
            y_ref[pl.ds(r + SLAB_PAD, SUBLANES), k1 * LANES:(k1 + 1) * LANES] = yi
        return carry

    lax.fori_loop(0, SLAB_PAD // SUBLANES, stage1, 0)

    for p in range(N_SLAB // 2):
        yp = y_ref[:, 2 * p * LANES:(2 * p + 2) * LANES].astype(bf16)
        g = jnp.dot(fm_ref[...], yp, preferred_element_type=f32)
        o_ref[pl.ds(2 * p, SLAB - 1, stride=N_SLAB), :] = g[:, 0:LANES]
        o_ref[pl.ds(2 * p + 1, SLAB - 1, stride=N_SLAB), :] = g[:, LANES:2 * LANES]


def _fourier(ab, ab_m, fm, twc, tws, batch, seq):
    n_blk = FOURIER_WIDTH // LANES
    x_spec = lambda off: pl.BlockSpec((None, seq, LANES), lambda b, j: (b, 0, j + off))
    m_spec = lambda off: pl.BlockSpec((N_META, LANES), lambda b, j: (0, j + off))
    return pl.pallas_call(
        _fourier_kernel,
        grid=(batch, n_blk),
        in_specs=[x_spec(0), x_spec(n_blk), m_spec(0), m_spec(n_blk),
                  _resident(fm.shape), _resident(twc.shape), _resident(tws.shape)],
        out_specs=pl.BlockSpec((None, seq, LANES), lambda b, j: (b, 0, j)),
        out_shape=jax.ShapeDtypeStruct((batch, seq, FOURIER_WIDTH), f32),
        scratch_shapes=[pltpu.VMEM((SLAB_PAD, LANES), f32)] * 4
                       + [pltpu.VMEM((2 * SLAB_PAD, N_SLAB * LANES), f32)],
        compiler_params=pltpu.CompilerParams(dimension_semantics=("arbitrary", "arbitrary"),
                                             vmem_limit_bytes=VMEM_LIMIT),
        name="fourier",
    )(ab, ab, ab_m, ab_m, fm, twc, tws)


def _fourier_constants():
    n = N_SLAB * SLAB
    k2 = np.arange(1, SLAB)[:, None]
    t2 = np.arange(SLAB_PAD)[None, :]
    ang = 2.0 * np.pi * ((k2 * t2) % SLAB) / SLAB
    valid = (t2 < SLAB)
    fm = np.concatenate([np.cos(ang) * valid, np.sin(ang) * valid], axis=1)
    k1 = np.arange(N_SLAB)[:, None]
    ang2 = 2.0 * np.pi * ((k1 * t2) % n) / n
    lane_bcast = np.ones((1, 1, LANES))
    twc = (np.cos(ang2) * valid)[:, :, None] * lane_bcast
    tws = (np.sin(ang2) * valid)[:, :, None] * lane_bcast
    return (jnp.asarray(fm, f32), jnp.asarray(twc, f32), jnp.asarray(tws, f32))


def _mix_out_kernel(h1_ref, hs_ref, gate_ref, four_ref, gl_ref, fb_ref, gf_ref, wo_ref,
                    g2_ref, w1_ref, w2_ref, gfin_ref, o_ref, mix_ref, act_ref):
    y_l = _rms_norm(hs_ref[...] * _gelu_tanh(gate_ref[...]), gl_ref[...])
    y_f = _rms_norm(four_ref[...] + fb_ref[...], gf_ref[...])
    mix_ref[:, 0:LRU_WIDTH] = y_l.astype(bf16)
    mix_ref[:, LRU_WIDTH:] = y_f.astype(bf16)
    h2 = h1_ref[...] + jnp.dot(mix_ref[...], wo_ref[...], preferred_element_type=f32)
    xn = _rms_norm(h2, g2_ref[...]).astype(bf16)
    h3 = h2 + 0.5 * _swiglu(xn, w1_ref, w2_ref, act_ref)
    o_ref[...] = _rms_norm(h3, gfin_ref[...])


def _mix_out(h1, hs, gate, four, gl, fb, gf, wo, g2, w1, w2, gfin, tile):
    rows = h1.shape[0]
    row_spec = lambda w: pl.BlockSpec((tile, w), lambda i: (i, 0))
    return pl.pallas_call(
        _mix_out_kernel,
        grid=(rows // tile,),
        in_specs=[row_spec(D_MODEL), row_spec(LRU_WIDTH), row_spec(LRU_WIDTH), row_spec(FOURIER_WIDTH),
                  _resident((1, LRU_WIDTH)), _resident((1, FOURIER_WIDTH)), _resident((1, FOURIER_WIDTH)),
                  _resident(wo.shape), _resident((1, D_MODEL)), _resident(w1.shape), _resident(w2.shape),
                  _resident((1, D_MODEL))],
        out_specs=row_spec(D_MODEL),
        out_shape=jax.ShapeDtypeStruct((rows, D_MODEL), f32),
        scratch_shapes=[pltpu.VMEM((tile, D_MODEL), bf16), pltpu.VMEM((tile, D_FF), bf16)],
        compiler_params=pltpu.CompilerParams(dimension_semantics=("arbitrary",), vmem_limit_bytes=VMEM_LIMIT),
        name="mix_out",
    )(h1, hs, gate, four, gl, fb, gf, wo, g2, w1, w2, gfin)


def _gate_weights(wa_f, wx_f, wa_b, wx_b, ba_f, bx_f, ba_b, bx_b):
    n_slab = LRU_WIDTH // LANES
    per = LANES // LRU_HEAD_DIM
    eye = jnp.eye(per, dtype=f32)

    def slab_bd(w):
        w = w.astype(f32).reshape(n_slab, per, LRU_HEAD_DIM, LRU_HEAD_DIM)
        return (eye[None, :, None, :, None] * w[:, :, :, None, :]).reshape(n_slab, LANES, LANES)

    wg = jnp.concatenate([slab_bd(w) for w in (wa_f, wx_f, wa_b, wx_b)], axis=-1)
    bg = jnp.concatenate([b.astype(f32).reshape(n_slab, 1, LANES) for b in (ba_f, bx_f, ba_b, bx_b)], axis=-1)
    return (0.5 * wg).astype(bf16), 0.5 * bg


def kernel(x, meta_tokens, norm_ffn1, w_ffn1_in, w_ffn1_out, norm_mix, w_in, conv_w, conv_b, lru_wa_fwd, lru_ba_fwd, lru_wx_fwd, lru_bx_fwd, lru_lambda_fwd, lru_wa_bwd, lru_ba_bwd, lru_wx_bwd, lru_bx_bwd, lru_lambda_bwd, fourier_w, fourier_b, norm_lru_out, norm_fourier_out, w_out, norm_ffn2, w_ffn2_in, w_ffn2_out, norm_final):
    batch, seq, d = x.shape
    assert (d, seq + N_META) == (D_MODEL, N_SLAB * SLAB) and norm_ffn1.shape[0] == 1
    l = 0
    row = lambda v: v.astype(f32).reshape(1, -1)

    mix = _fourier_channel_mix(fourier_w[l]).astype(bf16)
    win = w_in[l].astype(bf16)
    w1a, w2a = w_ffn1_in[l].astype(bf16), w_ffn1_out[l].astype(bf16)
    w1b, w2b = w_ffn2_in[l].astype(bf16), w_ffn2_out[l].astype(bf16)
    wo = w_out[l].astype(bf16)
    wg, bg = _gate_weights(lru_wa_fwd[l], lru_wx_fwd[l], lru_wa_bwd[l], lru_wx_bwd[l],
                           lru_ba_fwd[l], lru_bx_fwd[l], lru_ba_bwd[l], lru_bx_bwd[l])
    lam = jnp.stack([lru_lambda_fwd[l], lru_lambda_bwd[l]]).astype(f32)

    ffn_in = functools.partial(_ffn_in, g1=row(norm_ffn1[l]), w1=w1a, w2=w2a, g2=row(norm_mix[l]), win=win, mix=mix)
    h1, lru_x, gate, ab = ffn_in(x.reshape(batch * seq, d).astype(f32), tile=ROW_TILE)
    _, lru_m, _, ab_m = ffn_in(meta_tokens.astype(f32), tile=N_META)

    hsum = _lru(lru_x.reshape(batch, seq, LRU_WIDTH), lru_m, conv_w[l].astype(f32), row(conv_b[l]),
                wg, bg, lam, batch, seq)

    fm, twc, tws = _fourier_constants()
    four = _fourier(ab.reshape(batch, seq, 2 * FOURIER_WIDTH), ab_m, fm.astype(bf16), twc, tws, batch, seq)

    out = _mix_out(h1, hsum.reshape(batch * seq, LRU_WIDTH), gate, four.reshape(batch * seq, FOURIER_WIDTH),
                   row(norm_lru_out[l]), row(fourier_b[l]), row(norm_fourier_out[l]), wo,
                   row(norm_ffn2[l]), w1b, w2b, row(norm_final), tile=ROW_TILE)
    return out.reshape(batch, seq, d).astype(x.dtype)
```

```python
import math

import numpy as np
import jax
import jax.numpy as jnp
from jax import lax
from jax.experimental import pallas as pl
from jax.experimental.pallas import tpu as pltpu

D_MODEL = 1024
N_META = 16
LRU_WIDTH = 512
LRU_HEADS = 8
LRU_HEAD_DIM = 64
FOURIER_WIDTH = 512
FOURIER_GROUPS = 8
FOURIER_GROUP_DIM = 64
D_FF = 2816
EPS = 1e-6
RG_LRU_C = 8.0

LANES = 128
SUBLANES = 8
ROW_TILE = 512
MXU_COLS = 256
N_FF = D_FF // MXU_COLS
N_WIN = (2 * LRU_WIDTH + FOURIER_WIDTH) // MXU_COLS
N_WO = D_MODEL // MXU_COLS
VMEM_LIMIT = 60 * 1024 * 1024

N_SLAB = N_META
SLAB = 513
SLAB_PAD = 520
N_SEG = 16
SEG_LEN = 513
LRU_CHUNK = 456
SCAN_UNROLL = 9

f32 = jnp.float32
bf16 = jnp.bfloat16


def _rms_norm(x, g):
    return x * lax.rsqrt(jnp.mean(x * x, axis=-1, keepdims=True) + EPS) * g


def _sigmoid(x):
    return 0.5 * (1.0 + jnp.tanh(0.5 * x))


def _gelu_tanh(x):
    c = math.sqrt(2.0 / math.pi)
    return 0.5 * x * (1.0 + jnp.tanh(c * (x + 0.044715 * (x * x * x))))


def _swiglu(xn, w1g_s, w1u_s, w2_s, act_ref):
    for c in range(N_FF):
        g = jnp.dot(xn, w1g_s[c], preferred_element_type=f32)
        u = jnp.dot(xn, w1u_s[c], preferred_element_type=f32)
        act_ref[:, c * MXU_COLS:(c + 1) * MXU_COLS] = (g * _sigmoid(g) * u).astype(bf16)
    return jnp.dot(act_ref[...], w2_s[...], preferred_element_type=f32)


def _cast_ffn_chunk(i, w1g_ref, w1u_ref, w2_ref, w1g_s, w1u_s, w2_s):
    w1g_s[i] = w1g_ref[...].astype(bf16)
    w1u_s[i] = w1u_ref[...].astype(bf16)
    w2_s[pl.ds(pl.multiple_of(i * MXU_COLS, MXU_COLS), MXU_COLS), :] = w2_ref[...].astype(bf16)


def _ffn_weight_specs():
    last = N_FF - 1
    return [pl.BlockSpec((D_MODEL, MXU_COLS), lambda i: (0, jnp.minimum(i, last))),
            pl.BlockSpec((D_MODEL, MXU_COLS), lambda i: (0, N_FF + jnp.minimum(i, last))),
            pl.BlockSpec((MXU_COLS, D_MODEL), lambda i: (jnp.minimum(i, last), 0))]


def _ffn_weight_scratch():
    return [pltpu.VMEM((N_FF, D_MODEL, MXU_COLS), bf16), pltpu.VMEM((N_FF, D_MODEL, MXU_COLS), bf16),
            pltpu.VMEM((D_FF, D_MODEL), bf16)]


def _resident(shape):
    nd = len(shape)
    return pl.BlockSpec(shape, lambda *_: (0,) * nd, pipeline_mode=pl.Buffered(1))


def _chanmix_kernel(bd_ref, cd_ref, sd_ref, o_ref):
    hi = lax.Precision.HIGHEST
    for h in range(FOURIER_WIDTH // MXU_COLS):
        blk = slice(h * MXU_COLS, (h + 1) * MXU_COLS)
        bd = bd_ref[blk, blk]
        o_ref[h, :, 0:MXU_COLS] = jnp.dot(cd_ref[blk, blk], bd, preferred_element_type=f32, precision=hi)
        o_ref[h, :, MXU_COLS:] = jnp.dot(sd_ref[blk, blk], bd, preferred_element_type=f32, precision=hi)


def _fourier_channel_mix(fourier_w):
    g, n = FOURIER_GROUPS, FOURIER_GROUP_DIM
    eye = np.eye(g, dtype=np.float64)
    ang = 2.0 * np.pi * (np.outer(np.arange(n), np.arange(n)) % n) / n
    scale = 1.0 / math.sqrt((N_SLAB * SLAB) * n)
    cd = jnp.asarray(np.kron(eye, np.cos(ang)) * scale, f32)
    sd = jnp.asarray(np.kron(eye, -np.sin(ang)) * scale, f32)
    bd = (jnp.eye(g, dtype=f32)[:, None, :, None] * fourier_w.astype(f32)[:, :, None, :]).reshape(g * n, g * n)
    return pl.pallas_call(
        _chanmix_kernel,
        out_shape=jax.ShapeDtypeStruct((FOURIER_WIDTH // MXU_COLS, MXU_COLS, 2 * MXU_COLS), f32),
        name="fourier_channel_mix",
    )(bd, cd, sd)


def _ffn_in_kernel(x_ref, xm_ref, g1_ref, w1g_ref, w1u_ref, w2_ref, g2_ref, win_ref, mix_ref,
                   h1_ref, lru_ref, gate_ref, ab_ref, lrum_ref, abm_ref,
                   w1g_s, w1u_s, w2_s, win_s, act_ref, actm_ref):
    i = pl.program_id(0)

    @pl.when(i < N_FF)
    def _cast():
        _cast_ffn_chunk(i, w1g_ref, w1u_ref, w2_ref, w1g_s, w1u_s, w2_s)

    @pl.when(i < N_WIN)
    def _cast_win():
        win_s[i] = win_ref[...].astype(bf16)

    def rows_body(x, act):
        xn = _rms_norm(x, g1_ref[...]).astype(bf16)
        h1 = x + 0.5 * _swiglu(xn, w1g_s, w1u_s, w2_s, act)
        xn2 = _rms_norm(h1, g2_ref[...]).astype(bf16)
        u = [jnp.dot(xn2, win_s[c], preferred_element_type=f32) for c in range(N_WIN)]
        n_lru = LRU_WIDTH // MXU_COLS
        lru = jnp.concatenate(u[0:n_lru], axis=1)
        gate = jnp.concatenate(u[n_lru:2 * n_lru], axis=1)
        ab = [jnp.dot(v.astype(bf16), mix_ref[h], preferred_element_type=f32)
              for h, v in enumerate(u[2 * n_lru:])]
        a_b = jnp.concatenate([p[:, 0:MXU_COLS] for p in ab] + [p[:, MXU_COLS:] for p in ab], axis=1)
        return h1, lru, gate, a_b

    @pl.when(i == N_FF)
    def _meta():
        _, lru, _, a_b = rows_body(xm_ref[...], actm_ref)
        lrum_ref[...] = lru
        abm_ref[...] = a_b

    @pl.when(i > N_FF)
    def _tile():
        h1, lru, gate, a_b = rows_body(x_ref[...], act_ref)
        h1_ref[...] = h1
        lru_ref[...] = lru
        gate_ref[...] = gate
        ab_ref[...] = a_b


def _ffn_in(x2d, xm, g1, w_ffn_in, w_ffn_out, g2, w_in, mix, tile):
    rows = x2d.shape[0]
    n_meta = xm.shape[0]
    lead = N_FF + 1
    row_spec = lambda w: pl.BlockSpec((tile, w), lambda i: (jnp.maximum(i - lead, 0), 0))
    whole = lambda shape: pl.BlockSpec(shape, lambda i: (0,) * len(shape))
    return pl.pallas_call(
        _ffn_in_kernel,
        grid=(lead + rows // tile,),
        in_specs=[row_spec(D_MODEL), _resident(xm.shape), _resident((1, D_MODEL)), *_ffn_weight_specs(),
                  _resident((1, D_MODEL)),
                  pl.BlockSpec((D_MODEL, MXU_COLS), lambda i: (0, jnp.minimum(i, N_WIN - 1))),
                  _resident(mix.shape)],
        out_specs=[row_spec(D_MODEL), row_spec(LRU_WIDTH), row_spec(LRU_WIDTH), row_spec(2 * FOURIER_WIDTH),
                   whole((n_meta, LRU_WIDTH)), whole((n_meta, 2 * FOURIER_WIDTH))],
        out_shape=[jax.ShapeDtypeStruct((rows, D_MODEL), f32),
                   jax.ShapeDtypeStruct((rows, LRU_WIDTH), f32),
                   jax.ShapeDtypeStruct((rows, LRU_WIDTH), f32),
                   jax.ShapeDtypeStruct((rows, 2 * FOURIER_WIDTH), f32),
                   jax.ShapeDtypeStruct((n_meta, LRU_WIDTH), f32),
                   jax.ShapeDtypeStruct((n_meta, 2 * FOURIER_WIDTH), f32)],
        scratch_shapes=_ffn_weight_scratch()
                       + [pltpu.VMEM((N_WIN, D_MODEL, MXU_COLS), bf16),
                          pltpu.VMEM((tile, D_FF), bf16), pltpu.VMEM((n_meta, D_FF), bf16)],
        compiler_params=pltpu.CompilerParams(dimension_semantics=("arbitrary",), vmem_limit_bytes=VMEM_LIMIT),
        name="ffn_in",
    )(x2d, xm, g1, w_ffn_in, w_ffn_in, w_ffn_out, g2, w_in, mix)


def _lru_kernel(xl_ref, xm_ref, cw_ref, cb_ref, wg_ref, bg_ref, lam_ref, o_ref,
                seq_ref, af_ref, bf_ref, ab_ref, bb_ref, hf_ref, hb_ref, ends_ref, carry_ref):
    t_all = N_SEG * SEG_LEN
    zero8 = jnp.zeros((SUBLANES, LANES), f32)
    seq_ref[0:SUBLANES, :] = zero8
    seq_ref[SUBLANES:SUBLANES + N_META, :] = xm_ref[...]
    seq_ref[SUBLANES + N_META:SUBLANES + t_all, :] = xl_ref[...]
    seq_ref[SUBLANES + t_all:2 * SUBLANES + t_all, :] = zero8

    cw = cw_ref[...]
    cb = cb_ref[...]
    bg = bg_ref[...]
    lam = lam_ref[...]
    nlam = -lam
    sp = jnp.maximum(nlam, 0.0) + jnp.log1p(jnp.exp(-jnp.abs(nlam)))
    k_f = (-0.25 * RG_LRU_C) * sp[0:1, :]
    k_b = (-0.25 * RG_LRU_C) * sp[1:2, :]

    def gates(half_pre_r, half_pre_i, xc, k_dir):
        t = jnp.tanh(k_dir * jnp.tanh(half_pre_r) + k_dir)
        q = 1.0 / (1.0 - t)
        a = (1.0 + t) * q
        y = -t
        root = jnp.where(y > 0.0, y * lax.rsqrt(y), 0.0)
        return a, (root * q) * ((1.0 + jnp.tanh(half_pre_i)) * xc)

    def chunk_body(c, carry):
        r0 = pl.multiple_of(c * LRU_CHUNK, SUBLANES)
        xc = (cw[0:1, :] * seq_ref[pl.ds(r0 + 6, LRU_CHUNK), :] + cw[1:2, :] * seq_ref[pl.ds(r0 + 7, LRU_CHUNK), :]
              + cw[2:3, :] * seq_ref[pl.ds(r0 + 8, LRU_CHUNK), :] + cw[3:4, :] * seq_ref[pl.ds(r0 + 9, LRU_CHUNK), :]
              + cb)
        pre = jnp.dot(xc.astype(bf16), wg_ref[...], preferred_element_type=f32) + bg
        a_f, b_f = gates(pre[:, 0:LANES], pre[:, LANES:2 * LANES], xc, k_f)
        a_b, b_b = gates(pre[:, 2 * LANES:3 * LANES], pre[:, 3 * LANES:4 * LANES], xc, k_b)
        rows = pl.ds(r0, LRU_CHUNK)
        af_ref[rows, :] = a_f
        bf_ref[rows, :] = b_f
        ab_ref[rows, :] = a_b
        bb_ref[rows, :] = b_b
        return carry

    lax.fori_loop(0, t_all // LRU_CHUNK, chunk_body, 0)

    half = SUBLANES * SEG_LEN

    def seg_rows(j, hi):
        return pl.ds(j + hi * half, SUBLANES, stride=SEG_LEN)

    def pass1(j, st):
        hf0, hf1, pf0, pf1, hb0, hb1, pb0, pb1 = st
        jb = SEG_LEN - 1 - j
        a0 = af_ref[seg_rows(j, 0), :]
        a1 = af_ref[seg_rows(j, 1), :]
        hf0 = a0 * hf0 + bf_ref[seg_rows(j, 0), :]
        hf1 = a1 * hf1 + bf_ref[seg_rows(j, 1), :]
        pf0 = a0 * pf0
        pf1 = a1 * pf1
        c0 = ab_ref[seg_rows(jb, 0), :]
        c1 = ab_ref[seg_rows(jb, 1), :]
        hb0 = c0 * hb0 + bb_ref[seg_rows(jb, 0), :]
        hb1 = c1 * hb1 + bb_ref[seg_rows(jb, 1), :]
        pb0 = c0 * pb0
        pb1 = c1 * pb1
        return hf0, hf1, pf0, pf1, hb0, hb1, pb0, pb1

    one8 = jnp.ones((SUBLANES, LANES), f32)
    st = lax.fori_loop(0, SEG_LEN, pass1, (zero8, zero8, one8, one8, zero8, zero8, one8, one8),
                       unroll=SCAN_UNROLL)
    for idx, v in enumerate(st):
        ends_ref[idx * SUBLANES:(idx + 1) * SUBLANES, :] = v

    c = jnp.zeros((1, LANES), f32)
    for s in range(N_SEG):
        carry_ref[s:s + 1, :] = c
        c = ends_ref[s:s + 1, :] + ends_ref[N_SEG + s:N_SEG + s + 1, :] * c
    c = jnp.zeros((1, LANES), f32)
    for s in range(N_SEG - 1, -1, -1):
        carry_ref[N_SEG + s:N_SEG + s + 1, :] = c
        c = ends_ref[2 * N_SEG + s:2 * N_SEG + s + 1, :] + ends_ref[3 * N_SEG + s:3 * N_SEG + s + 1, :] * c

    def pass2(j, st):
        hf0, hf1, hb0, hb1 = st
        jb = SEG_LEN - 1 - j
        hf0 = af_ref[seg_rows(j, 0), :] * hf0 + bf_ref[seg_rows(j, 0), :]
        hf1 = af_ref[seg_rows(j, 1), :] * hf1 + bf_ref[seg_rows(j, 1), :]
        hf_ref[seg_rows(j, 0), :] = hf0
        hf_ref[seg_rows(j, 1), :] = hf1
        hb0 = ab_ref[seg_rows(jb, 0), :] * hb0 + bb_ref[seg_rows(jb, 0), :]
        hb1 = ab_ref[seg_rows(jb, 1), :] * hb1 + bb_ref[seg_rows(jb, 1), :]
        hb_ref[seg_rows(jb, 0), :] = hb0
        hb_ref[seg_rows(jb, 1), :] = hb1
        return hf0, hf1, hb0, hb1

    lax.fori_loop(0, SEG_LEN, pass2,
                  (carry_ref[0:SUBLANES, :], carry_ref[SUBLANES:2 * SUBLANES, :],
                   carry_ref[2 * SUBLANES:3 * SUBLANES, :], carry_ref[3 * SUBLANES:4 * SUBLANES, :]),
                  unroll=SCAN_UNROLL)

    o_ref[...] = hf_ref[N_META:t_all, :] + hb_ref[N_META:t_all, :]


def _lru(lru_x, lru_meta, conv_w, conv_b, wg, bg, lam, batch, seq):
    t_all = N_META + seq
    n_slab = LRU_WIDTH // LANES
    slab = lambda r: pl.BlockSpec((r, LANES), lambda b, s: (0, s))
    return pl.pallas_call(
        _lru_kernel,
        grid=(batch, n_slab),
        in_specs=[pl.BlockSpec((None, seq, LANES), lambda b, s: (b, 0, s)),
                  slab(N_META), slab(4), slab(1),
                  pl.BlockSpec((None, LANES, 4 * LANES), lambda b, s: (s, 0, 0)),
                  pl.BlockSpec((None, 1, 4 * LANES), lambda b, s: (s, 0, 0)),
                  slab(2)],
        out_specs=pl.BlockSpec((None, seq, LANES), lambda b, s: (b, 0, s)),
        out_shape=jax.ShapeDtypeStruct((batch, seq, LRU_WIDTH), f32),
        scratch_shapes=[pltpu.VMEM((t_all + 2 * SUBLANES, LANES), f32)]
                       + [pltpu.VMEM((t_all, LANES), f32)] * 6
                       + [pltpu.VMEM((4 * N_SEG, LANES), f32), pltpu.VMEM((2 * N_SEG, LANES), f32)],
        compiler_params=pltpu.CompilerParams(dimension_semantics=("arbitrary", "arbitrary"),
                                             vmem_limit_bytes=VMEM_LIMIT),
        name="lru",
    )(lru_x, lru_meta, conv_w, conv_b, wg, bg, lam)


def _cmul_const(x, c, s):
    re, im = x
    tol = 1e-12
    if abs(s) < tol:
        return (re, im) if c > 0 else (-re, -im)
    if abs(c) < tol:
        return (-im, re) if s > 0 else (im, -re)
    if abs(abs(c) - abs(s)) < tol:
        m = abs(c)
        sc, ss = (1.0 if c > 0 else -1.0), (1.0 if s > 0 else -1.0)
        return (m * (sc * re - ss * im), m * (ss * re + sc * im))
    return (c * re - s * im, s * re + c * im)


def _fft(xs):
    n = len(xs)
    if n == 1:
        return xs
    ev = _fft(xs[0::2])
    od = _fft(xs[1::2])
    out = [None] * n
    for k in range(n // 2):
        ang = -2.0 * math.pi * k / n
        t = _cmul_const(od[k], math.cos(ang), math.sin(ang))
        out[k] = (ev[k][0] + t[0], ev[k][1] + t[1])
        out[k + n // 2] = (ev[k][0] - t[0], ev[k][1] - t[1])
    return out


def _fourier_kernel(xa_ref, xb_ref, ma_ref, mb_ref, fm_ref, twc_ref, tws_ref, o_ref,
                    ha_ref, hb_ref, ta_ref, tb_ref, y_ref):
    tail0 = (N_SLAB - 1) * SLAB - N_META
    for m_ref, x_ref, h_ref, t_ref in ((ma_ref, xa_ref, ha_ref, ta_ref), (mb_ref, xb_ref, hb_ref, tb_ref)):
        h_ref[0:N_META, :] = m_ref[...]
        h_ref[N_META:SLAB_PAD, :] = x_ref[0:SLAB_PAD - N_META, :]
        t_ref[SLAB - 1:SLAB_PAD, :] = jnp.zeros((SLAB_PAD - SLAB + 1, LANES), f32)
        t_ref[0:SLAB - 1, :] = x_ref[tail0:tail0 + SLAB - 1, :]
        t_ref[SLAB - 1:SLAB, :] = x_ref[tail0 + SLAB - 1:tail0 + SLAB, :]

    def stage1(it, carry):
        r = pl.multiple_of(it * SUBLANES, SUBLANES)
        rows = pl.ds(r, SUBLANES)

        def slab_rows(x_ref, h_ref, t_ref, t1):
            if t1 == 0:
                return h_ref[rows, :]
            if t1 == N_SLAB - 1:
                return t_ref[rows, :]
            return x_ref[pl.ds(r + (t1 * SLAB - N_META), SUBLANES), :]

        zs = [(slab_rows(xa_ref, ha_ref, ta_ref, t1), slab_rows(xb_ref, hb_ref, tb_ref, t1))
              for t1 in range(N_SLAB)]
        ys = _fft(zs)
        for k1 in range(N_SLAB):
            yr, yi = ys[k1]
            if k1 > 0:
                cw = twc_ref[k1, rows, :]
                sw = tws_ref[k1, rows, :]
                yr, yi = yr * cw + yi * sw, yi * cw - yr * sw
            y_ref[rows, k1 * LANES:(k1 + 1) * LANES] = yr
            y_ref[pl.ds(r + SLAB_PAD, SUBLANES), k1 * LANES:(k1 + 1) * LANES] = yi
        return carry

    lax.fori_loop(0, SLAB_PAD // SUBLANES, stage1, 0)

    for p in range(N_SLAB // 2):
        yp = y_ref[:, 2 * p * LANES:(2 * p + 2) * LANES].astype(bf16)
        g = jnp.dot(fm_ref[...], yp, preferred_element_type=f32)
        o_ref[pl.ds(2 * p, SLAB - 1, stride=N_SLAB), :] = g[:, 0:LANES]
        o_ref[pl.ds(2 * p + 1, SLAB - 1, stride=N_SLAB), :] = g[:, LANES:2 * LANES]


def _fourier(ab, ab_m, fm, twc, tws, batch, seq):
    n_blk = FOURIER_WIDTH // LANES
    x_spec = lambda off: pl.BlockSpec((None, seq, LANES), lambda b, j: (b, 0, j + off))
    m_spec = lambda off: pl.BlockSpec((N_META, LANES), lambda b, j: (0, j + off))
    return pl.pallas_call(
        _fourier_kernel,
        grid=(batch, n_blk),
        in_specs=[x_spec(0), x_spec(n_blk), m_spec(0), m_spec(n_blk),
                  _resident(fm.shape), _resident(twc.shape), _resident(tws.shape)],
        out_specs=pl.BlockSpec((None, seq, LANES), lambda b, j: (b, 0, j)),
        out_shape=jax.ShapeDtypeStruct((batch, seq, FOURIER_WIDTH), f32),
        scratch_shapes=[pltpu.VMEM((SLAB_PAD, LANES), f32)] * 4
                       + [pltpu.VMEM((2 * SLAB_PAD, N_SLAB * LANES), f32)],
        compiler_params=pltpu.CompilerParams(dimension_semantics=("arbitrary", "arbitrary"),
                                             vmem_limit_bytes=VMEM_LIMIT),
        name="fourier",
    )(ab, ab, ab_m, ab_m, fm, twc, tws)


def _fourier_constants():
    n = N_SLAB * SLAB
    k2 = np.arange(1, SLAB)[:, None]
    t2 = np.arange(SLAB_PAD)[None, :]
    ang = 2.0 * np.pi * ((k2 * t2) % SLAB) / SLAB
    valid = (t2 < SLAB)
    fm = np.concatenate([np.cos(ang) * valid, np.sin(ang) * valid], axis=1)
    k1 = np.arange(N_SLAB)[:, None]
    ang2 = 2.0 * np.pi * ((k1 * t2) % n) / n
    lane_bcast = np.ones((1, 1, LANES))
    twc = (np.cos(ang2) * valid)[:, :, None] * lane_bcast
    tws = (np.sin(ang2) * valid)[:, :, None] * lane_bcast
    return (jnp.asarray(fm, f32), jnp.asarray(twc, f32), jnp.asarray(tws, f32))


def _mix_out_kernel(h1_ref, hs_ref, gate_ref, four_ref, gl_ref, fb_ref, gf_ref, wo_ref,
                    g2_ref, w1g_ref, w1u_ref, w2_ref, gfin_ref, o_ref,
                    wo_s, w1g_s, w1u_s, w2_s, mix_ref, act_ref):
    i = pl.program_id(0)

    @pl.when(i < N_FF)
    def _cast():
        _cast_ffn_chunk(i, w1g_ref, w1u_ref, w2_ref, w1g_s, w1u_s, w2_s)

    @pl.when(i < N_WO)
    def _cast_wo():
        wo_s[i] = wo_ref[...].astype(bf16)

    @pl.when(i >= N_FF)
    def _tile():
        y_l = _rms_norm(hs_ref[...] * _gelu_tanh(gate_ref[...]), gl_ref[...])
        y_f = _rms_norm(four_ref[...] + fb_ref[...], gf_ref[...])
        mix_ref[:, 0:LRU_WIDTH] = y_l.astype(bf16)
        mix_ref[:, LRU_WIDTH:] = y_f.astype(bf16)
        mixed = mix_ref[...]
        proj = jnp.concatenate([jnp.dot(mixed, wo_s[c], preferred_element_type=f32) for c in range(N_WO)], axis=1)
        h2 = h1_ref[...] + proj
        xn = _rms_norm(h2, g2_ref[...]).astype(bf16)
        h3 = h2 + 0.5 * _swiglu(xn, w1g_s, w1u_s, w2_s, act_ref)
        o_ref[...] = _rms_norm(h3, gfin_ref[...])


def _mix_out(h1, hs, gate, four, gl, fb, gf, w_out, g2, w_ffn_in, w_ffn_out, gfin, tile):
    rows = h1.shape[0]
    row_spec = lambda w: pl.BlockSpec((tile, w), lambda i: (jnp.maximum(i - N_FF, 0), 0))
    return pl.pallas_call(
        _mix_out_kernel,
        grid=(N_FF + rows // tile,),
        in_specs=[row_spec(D_MODEL), row_spec(LRU_WIDTH), row_spec(LRU_WIDTH), row_spec(FOURIER_WIDTH),
                  _resident((1, LRU_WIDTH)), _resident((1, FOURIER_WIDTH)), _resident((1, FOURIER_WIDTH)),
                  pl.BlockSpec((D_MODEL, MXU_COLS), lambda i: (0, jnp.minimum(i, N_WO - 1))),
                  _resident((1, D_MODEL)), *_ffn_weight_specs(), _resident((1, D_MODEL))],
        out_specs=row_spec(D_MODEL),
        out_shape=jax.ShapeDtypeStruct((rows, D_MODEL), f32),
        scratch_shapes=[pltpu.VMEM((N_WO, D_MODEL, MXU_COLS), bf16)] + _ffn_weight_scratch()
                       + [pltpu.VMEM((tile, D_MODEL), bf16), pltpu.VMEM((tile, D_FF), bf16)],
        compiler_params=pltpu.CompilerParams(dimension_semantics=("arbitrary",), vmem_limit_bytes=VMEM_LIMIT),
        name="mix_out",
    )(h1, hs, gate, four, gl, fb, gf, w_out, g2, w_ffn_in, w_ffn_in, w_ffn_out, gfin)


def _gate_weights(wa_f, wx_f, wa_b, wx_b, ba_f, bx_f, ba_b, bx_b):
    n_slab = LRU_WIDTH // LANES
    per = LANES // LRU_HEAD_DIM
    eye = jnp.eye(per, dtype=f32)

    def slab_bd(w):
        w = w.astype(f32).reshape(n_slab, per, LRU_HEAD_DIM, LRU_HEAD_DIM)
        return (eye[None, :, None, :, None] * w[:, :, :, None, :]).reshape(n_slab, LANES, LANES)

    wg = jnp.concatenate([slab_bd(w) for w in (wa_f, wx_f, wa_b, wx_b)], axis=-1)
    bg = jnp.concatenate([b.astype(f32).reshape(n_slab, 1, LANES) for b in (ba_f, bx_f, ba_b, bx_b)], axis=-1)
    return (0.5 * wg).astype(bf16), 0.5 * bg


def kernel(x, meta_tokens, norm_ffn1, w_ffn1_in, w_ffn1_out, norm_mix, w_in, conv_w, conv_b, lru_wa_fwd, lru_ba_fwd, lru_wx_fwd, lru_bx_fwd, lru_lambda_fwd, lru_wa_bwd, lru_ba_bwd, lru_wx_bwd, lru_bx_bwd, lru_lambda_bwd, fourier_w, fourier_b, norm_lru_out, norm_fourier_out, w_out, norm_ffn2, w_ffn2_in, w_ffn2_out, norm_final):
    batch, seq, d = x.shape
    assert (d, seq + N_META) == (D_MODEL, N_SLAB * SLAB) and norm_ffn1.shape[0] == 1
    l = 0
    row = lambda v: v.astype(f32).reshape(1, -1)

    mix = _fourier_channel_mix(fourier_w[l]).astype(bf16)
    wg, bg = _gate_weights(lru_wa_fwd[l], lru_wx_fwd[l], lru_wa_bwd[l], lru_wx_bwd[l],
                           lru_ba_fwd[l], lru_bx_fwd[l], lru_ba_bwd[l], lru_bx_bwd[l])
    lam = jnp.stack([lru_lambda_fwd[l], lru_lambda_bwd[l]]).astype(f32)

    h1, lru_x, gate, ab, lru_m, ab_m = _ffn_in(
        x.reshape(batch * seq, d).astype(f32), meta_tokens.astype(f32), row(norm_ffn1[l]),
        w_ffn1_in[l].astype(f32), w_ffn1_out[l].astype(f32), row(norm_mix[l]), w_in[l].astype(f32), mix,
        tile=ROW_TILE)

    hsum = _lru(lru_x.reshape(batch, seq, LRU_WIDTH), lru_m, conv_w[l].astype(f32), row(conv_b[l]),
                wg, bg, lam, batch, seq)

    fm, twc, tws = _fourier_constants()
    four = _fourier(ab.reshape(batch, seq, 2 * FOURIER_WIDTH), ab_m, fm.astype(bf16), twc, tws, batch, seq)

    out = _mix_out(h1, hsum.reshape(batch * seq, LRU_WIDTH), gate, four.reshape(batch * seq, FOURIER_WIDTH),
                   row(norm_lru_out[l]), row(fourier_b[l]), row(norm_fourier_out[l]), w_out[l].astype(f32),
                   row(norm_ffn2[l]), w_ffn2_in[l].astype(f32), w_ffn2_out[l].astype(f32), row(norm_final),
                   tile=ROW_TILE)
    return out.reshape(batch, seq, d).astype(x.dtype)
```

```python
import math

import numpy as np
import jax
import jax.numpy as jnp
from jax import lax
from jax.experimental import pallas as pl
from jax.experimental.pallas import tpu as pltpu

D_MODEL = 1024
N_META = 16
LRU_WIDTH = 512
LRU_HEADS = 8
LRU_HEAD_DIM = 64
FOURIER_WIDTH = 512
FOURIER_GROUPS = 8
FOURIER_GROUP_DIM = 64
D_FF = 2816
EPS = 1e-6
RG_LRU_C = 8.0

LANES = 128
SUBLANES = 8
ROW_TILE = 512
TILES_PER_BATCH = 16
STEPS_PER_BATCH = TILES_PER_BATCH + 2
GATE_PIECES = 8
SEQ_HEAD = 24
MXU_COLS = 256
N_FF = D_FF // MXU_COLS
N_WIN = (2 * LRU_WIDTH + FOURIER_WIDTH) // MXU_COLS
N_WO = D_MODEL // MXU_COLS
VMEM_LIMIT = 60 * 1024 * 1024

N_SLAB = N_META
SLAB = 513
SLAB_PAD = 520
N_SEG = 16
SEG_LEN = 513
SCAN_UNROLL = 9

f32 = jnp.float32
bf16 = jnp.bfloat16


def _rms_norm(x, g):
    return x * lax.rsqrt(jnp.mean(x * x, axis=-1, keepdims=True) + EPS) * g


def _sigmoid(x):
    return 0.5 * (1.0 + jnp.tanh(0.5 * x))


def _gelu_tanh(x):
    c = math.sqrt(2.0 / math.pi)
    return 0.5 * x * (1.0 + jnp.tanh(c * (x + 0.044715 * (x * x * x))))


def _swiglu(xn, w1g_s, w1u_s, w2_s, act_ref, between=None):
    for c in range(N_FF):
        g = jnp.dot(xn, w1g_s[c], preferred_element_type=f32)
        u = jnp.dot(xn, w1u_s[c], preferred_element_type=f32)
        act_ref[:, c * MXU_COLS:(c + 1) * MXU_COLS] = (g * _sigmoid(g) * u).astype(bf16)
        if between is not None:
            between(c)
    return jnp.dot(act_ref[...], w2_s[...], preferred_element_type=f32)


def _cast_ffn_chunk(i, w1g_ref, w1u_ref, w2_ref, w1g_s, w1u_s, w2_s):
    w1g_s[i] = w1g_ref[...].astype(bf16)
    w1u_s[i] = w1u_ref[...].astype(bf16)
    w2_s[pl.ds(pl.multiple_of(i * MXU_COLS, MXU_COLS), MXU_COLS), :] = w2_ref[...].astype(bf16)


def _ffn_weight_specs():
    last = N_FF - 1
    return [pl.BlockSpec((D_MODEL, MXU_COLS), lambda i: (0, jnp.minimum(i, last))),
            pl.BlockSpec((D_MODEL, MXU_COLS), lambda i: (0, N_FF + jnp.minimum(i, last))),
            pl.BlockSpec((MXU_COLS, D_MODEL), lambda i: (jnp.minimum(i, last), 0))]


def _ffn_weight_scratch():
    return [pltpu.VMEM((N_FF, D_MODEL, MXU_COLS), bf16), pltpu.VMEM((N_FF, D_MODEL, MXU_COLS), bf16),
            pltpu.VMEM((D_FF, D_MODEL), bf16)]


def _resident(shape):
    nd = len(shape)
    return pl.BlockSpec(shape, lambda *_: (0,) * nd, pipeline_mode=pl.Buffered(1))


def _chanmix_kernel(bd_ref, cd_ref, sd_ref, o_ref):
    hi = lax.Precision.HIGHEST
    for h in range(FOURIER_WIDTH // MXU_COLS):
        blk = slice(h * MXU_COLS, (h + 1) * MXU_COLS)
        bd = bd_ref[blk, blk]
        o_ref[h, :, 0:MXU_COLS] = jnp.dot(cd_ref[blk, blk], bd, preferred_element_type=f32, precision=hi)
        o_ref[h, :, MXU_COLS:] = jnp.dot(sd_ref[blk, blk], bd, preferred_element_type=f32, precision=hi)


def _fourier_channel_mix(fourier_w):
    g, n = FOURIER_GROUPS, FOURIER_GROUP_DIM
    eye = np.eye(g, dtype=np.float64)
    ang = 2.0 * np.pi * (np.outer(np.arange(n), np.arange(n)) % n) / n
    scale = 1.0 / math.sqrt((N_SLAB * SLAB) * n)
    cd = jnp.asarray(np.kron(eye, np.cos(ang)) * scale, f32)
    sd = jnp.asarray(np.kron(eye, -np.sin(ang)) * scale, f32)
    bd = (jnp.eye(g, dtype=f32)[:, None, :, None] * fourier_w.astype(f32)[:, :, None, :]).reshape(g * n, g * n)
    return pl.pallas_call(
        _chanmix_kernel,
        out_shape=jax.ShapeDtypeStruct((FOURIER_WIDTH // MXU_COLS, MXU_COLS, 2 * MXU_COLS), f32),
        name="fourier_channel_mix",
    )(bd, cd, sd)


def _lru_gate_rows(seq_s, r0, n, cw, cb, wg_ref, bg_ref, k_f, k_b, outs):
    af_ref, bf_ref, ab_ref, bb_ref = outs
    xc = (cw[0:1, :] * seq_s[r0 + 6:r0 + 6 + n, :] + cw[1:2, :] * seq_s[r0 + 7:r0 + 7 + n, :]
          + cw[2:3, :] * seq_s[r0 + 8:r0 + 8 + n, :] + cw[3:4, :] * seq_s[r0 + 9:r0 + 9 + n, :] + cb)

    def gates(half_pre_r, half_pre_i, x, k_dir):
        t = jnp.tanh(k_dir * jnp.tanh(half_pre_r) + k_dir)
        q = 1.0 / (1.0 - t)
        a = (1.0 + t) * q
        y = -t
        root = jnp.where(y > 0.0, y * lax.rsqrt(y), 0.0)
        return a, (root * q) * ((1.0 + jnp.tanh(half_pre_i)) * x)

    for s in range(LRU_WIDTH // LANES):
        lanes = slice(s * LANES, (s + 1) * LANES)
        x = xc[:, lanes]
        pre = jnp.dot(x.astype(bf16), wg_ref[s], preferred_element_type=f32) + bg_ref[s]
        a_f, b_f = gates(pre[:, 0:LANES], pre[:, LANES:2 * LANES], x, k_f[:, lanes])
        a_b, b_b = gates(pre[:, 2 * LANES:3 * LANES], pre[:, 3 * LANES:4 * LANES], x, k_b[:, lanes])
        af_ref[r0:r0 + n, lanes] = a_f
        bf_ref[r0:r0 + n, lanes] = b_f
        ab_ref[r0:r0 + n, lanes] = a_b
        bb_ref[r0:r0 + n, lanes] = b_b


def _ffn_in_kernel(x_ref, xm_ref, g1_ref, w1g_ref, w1u_ref, w2_ref, g2_ref, win_ref, mix_ref,
                   cw_ref, cb_ref, wg_ref, bg_ref, lam_ref,
                   h1_ref, gate_ref, ab_ref, abm_ref, af_ref, bf_ref, abk_ref, bb_ref,
                   w1g_s, w1u_s, w2_s, win_s, act_ref, actm_ref, seq_s, carry_s, lrum_s):
    i = pl.program_id(0)
    q = jnp.maximum(i - (N_FF + 1), 0)
    j = q % STEPS_PER_BATCH
    in_batch = i > N_FF
    gate_outs = (af_ref, bf_ref, abk_ref, bb_ref)

    @pl.when(i < N_FF)
    def _cast():
        _cast_ffn_chunk(i, w1g_ref, w1u_ref, w2_ref, w1g_s, w1u_s, w2_s)

    @pl.when(i < N_WIN)
    def _cast_win():
        win_s[i] = win_ref[...].astype(bf16)

    @pl.when(i == 0)
    def _init():
        seq_s[...] = jnp.zeros(seq_s.shape, f32)
        carry_s[...] = jnp.zeros(carry_s.shape, f32)

    def rows_body(x, act, between=None):
        xn = _rms_norm(x, g1_ref[...]).astype(bf16)
        h1 = x + 0.5 * _swiglu(xn, w1g_s, w1u_s, w2_s, act, between)
        xn2 = _rms_norm(h1, g2_ref[...]).astype(bf16)
        u = [jnp.dot(xn2, win_s[c], preferred_element_type=f32) for c in range(N_WIN)]
        n_lru = LRU_WIDTH // MXU_COLS
        lru = jnp.concatenate(u[0:n_lru], axis=1)
        gate = jnp.concatenate(u[n_lru:2 * n_lru], axis=1)
        ab = [jnp.dot(v.astype(bf16), mix_ref[h], preferred_element_type=f32)
              for h, v in enumerate(u[2 * n_lru:])]
        a_b = jnp.concatenate([p[:, 0:MXU_COLS] for p in ab] + [p[:, MXU_COLS:] for p in ab], axis=1)
        return h1, lru, gate, a_b

    def gate_params():
        nlam = -lam_ref[...]
        sp = jnp.maximum(nlam, 0.0) + jnp.log1p(jnp.exp(-jnp.abs(nlam)))
        return cw_ref[...], cb_ref[...], (-0.25 * RG_LRU_C) * sp[0:1, :], (-0.25 * RG_LRU_C) * sp[1:2, :]

    @pl.when(i == N_FF)
    def _meta():
        _, lru, _, a_b = rows_body(xm_ref[...], actm_ref)
        lrum_s[...] = lru
        abm_ref[...] = a_b

    @pl.when(in_batch & (j < TILES_PER_BATCH))
    def _tile():
        cw, cb, k_f, k_b = gate_params()
        piece = ROW_TILE // GATE_PIECES

        def between(c):
            if c < GATE_PIECES:
                _lru_gate_rows(seq_s, c * piece, piece, cw, cb, wg_ref, bg_ref, k_f, k_b, gate_outs)

        h1, lru, gate, a_b = rows_body(x_ref[...], act_ref, between)
        h1_ref[...] = h1
        gate_ref[...] = gate
        ab_ref[...] = a_b
        head = jnp.concatenate([jnp.zeros((SUBLANES, LRU_WIDTH), f32), lrum_s[...]], axis=0)
        seq_s[0:SEQ_HEAD, :] = jnp.where(j == 0, head, carry_s[...])
        seq_s[SEQ_HEAD:SEQ_HEAD + ROW_TILE, :] = lru
        carry_s[...] = lru[ROW_TILE - SEQ_HEAD:ROW_TILE, :]

    @pl.when(in_batch & (j == TILES_PER_BATCH))
    def _last_block():
        cw, cb, k_f, k_b = gate_params()
        _lru_gate_rows(seq_s, 0, ROW_TILE, cw, cb, wg_ref, bg_ref, k_f, k_b, gate_outs)
        seq_s[0:SEQ_HEAD, :] = carry_s[...]
        seq_s[SEQ_HEAD:SEQ_HEAD + SUBLANES, :] = jnp.zeros((SUBLANES, LRU_WIDTH), f32)

    @pl.when(in_batch & (j == TILES_PER_BATCH + 1))
    def _tail_block():
        cw, cb, k_f, k_b = gate_params()
        _lru_gate_rows(seq_s, 0, N_META, cw, cb, wg_ref, bg_ref, k_f, k_b, gate_outs)


def _ffn_in(x3d, xm, g1, w_ffn_in, w_ffn_out, g2, w_in, mix, conv_w, conv_b, wg, bg, lam):
    batch, seq, _ = x3d.shape
    assert seq == TILES_PER_BATCH * ROW_TILE
    n_meta = xm.shape[0]
    lead = N_FF + 1

    def tile_index(i):
        q = jnp.maximum(i - lead, 0)
        return q // STEPS_PER_BATCH, jnp.minimum(q % STEPS_PER_BATCH, TILES_PER_BATCH - 1)

    def block_index(i):
        q = jnp.maximum(i - lead, 0)
        return q // STEPS_PER_BATCH, jnp.clip(q % STEPS_PER_BATCH - 1, 0, TILES_PER_BATCH)

    row_spec = lambda w: pl.BlockSpec((None, ROW_TILE, w), lambda i: (*tile_index(i), 0))
    seq_spec = pl.BlockSpec((None, ROW_TILE, LRU_WIDTH), lambda i: (*block_index(i), 0))
    whole = lambda shape: pl.BlockSpec(shape, lambda i: (0,) * len(shape))
    coeff = jax.ShapeDtypeStruct((batch, n_meta + seq, LRU_WIDTH), f32)
    return pl.pallas_call(
        _ffn_in_kernel,
        grid=(lead + batch * STEPS_PER_BATCH,),
        in_specs=[row_spec(D_MODEL), _resident(xm.shape), _resident((1, D_MODEL)), *_ffn_weight_specs(),
                  _resident((1, D_MODEL)),
                  pl.BlockSpec((D_MODEL, MXU_COLS), lambda i: (0, jnp.minimum(i, N_WIN - 1))),
                  _resident(mix.shape), _resident(conv_w.shape), _resident(conv_b.shape),
                  _resident(wg.shape), _resident(bg.shape), _resident(lam.shape)],
        out_specs=[row_spec(D_MODEL), row_spec(LRU_WIDTH), row_spec(2 * FOURIER_WIDTH),
                   whole((n_meta, 2 * FOURIER_WIDTH)), seq_spec, seq_spec, seq_spec, seq_spec],
        out_shape=[jax.ShapeDtypeStruct((batch, seq, D_MODEL), f32),
                   jax.ShapeDtypeStruct((batch, seq, LRU_WIDTH), f32),
                   jax.ShapeDtypeStruct((batch, seq, 2 * FOURIER_WIDTH), f32),
                   jax.ShapeDtypeStruct((n_meta, 2 * FOURIER_WIDTH), f32),
                   coeff, coeff, coeff, coeff],
        scratch_shapes=_ffn_weight_scratch()
                       + [pltpu.VMEM((N_WIN, D_MODEL, MXU_COLS), bf16),
                          pltpu.VMEM((ROW_TILE, D_FF), bf16), pltpu.VMEM((n_meta, D_FF), bf16),
                          pltpu.VMEM((SEQ_HEAD + ROW_TILE, LRU_WIDTH), f32),
                          pltpu.VMEM((SEQ_HEAD, LRU_WIDTH), f32), pltpu.VMEM((n_meta, LRU_WIDTH), f32)],
        compiler_params=pltpu.CompilerParams(dimension_semantics=("arbitrary",), vmem_limit_bytes=VMEM_LIMIT),
        name="ffn_in",
    )(x3d, xm, g1, w_ffn_in, w_ffn_in, w_ffn_out, g2, w_in, mix, conv_w, conv_b, wg, bg, lam)


def _lru_kernel(af_ref, bf_ref, ab_ref, bb_ref, o_ref, hf_ref, hb_ref, ends_ref, carry_ref):
    t_all = N_SEG * SEG_LEN
    zero8 = jnp.zeros((SUBLANES, LANES), f32)
    half = SUBLANES * SEG_LEN

    def seg_rows(j, hi):
        return pl.ds(j + hi * half, SUBLANES, stride=SEG_LEN)

    def pass1(j, st):
        hf0, hf1, pf0, pf1, hb0, hb1, pb0, pb1 = st
        jb = SEG_LEN - 1 - j
        a0 = af_ref[seg_rows(j, 0), :]
        a1 = af_ref[seg_rows(j, 1), :]
        hf0 = a0 * hf0 + bf_ref[seg_rows(j, 0), :]
        hf1 = a1 * hf1 + bf_ref[seg_rows(j, 1), :]
        pf0 = a0 * pf0
        pf1 = a1 * pf1
        c0 = ab_ref[seg_rows(jb, 0), :]
        c1 = ab_ref[seg_rows(jb, 1), :]
        hb0 = c0 * hb0 + bb_ref[seg_rows(jb, 0), :]
        hb1 = c1 * hb1 + bb_ref[seg_rows(jb, 1), :]
        pb0 = c0 * pb0
        pb1 = c1 * pb1
        return hf0, hf1, pf0, pf1, hb0, hb1, pb0, pb1

    one8 = jnp.ones((SUBLANES, LANES), f32)
    st = lax.fori_loop(0, SEG_LEN, pass1, (zero8, zero8, one8, one8, zero8, zero8, one8, one8),
                       unroll=SCAN_UNROLL)
    for idx, v in enumerate(st):
        ends_ref[idx * SUBLANES:(idx + 1) * SUBLANES, :] = v

    c = jnp.zeros((1, LANES), f32)
    for s in range(N_SEG):
        carry_ref[s:s + 1, :] = c
        c = ends_ref[s:s + 1, :] + ends_ref[N_SEG + s:N_SEG + s + 1, :] * c
    c = jnp.zeros((1, LANES), f32)
    for s in range(N_SEG - 1, -1, -1):
        carry_ref[N_SEG + s:N_SEG + s + 1, :] = c
        c = ends_ref[2 * N_SEG + s:2 * N_SEG + s + 1, :] + ends_ref[3 * N_SEG + s:3 * N_SEG + s + 1, :] * c

    def pass2(j, st):
        hf0, hf1, hb0, hb1 = st
        jb = SEG_LEN - 1 - j
        hf0 = af_ref[seg_rows(j, 0), :] * hf0 + bf_ref[seg_rows(j, 0), :]
        hf1 = af_ref[seg_rows(j, 1), :] * hf1 + bf_ref[seg_rows(j, 1), :]
        hf_ref[seg_rows(j, 0), :] = hf0
        hf_ref[seg_rows(j, 1), :] = hf1
        hb0 = ab_ref[seg_rows(jb, 0), :] * hb0 + bb_ref[seg_rows(jb, 0), :]
        hb1 = ab_ref[seg_rows(jb, 1), :] * hb1 + bb_ref[seg_rows(jb, 1), :]
        hb_ref[seg_rows(jb, 0), :] = hb0
        hb_ref[seg_rows(jb, 1), :] = hb1
        return hf0, hf1, hb0, hb1

    lax.fori_loop(0, SEG_LEN, pass2,
                  (carry_ref[0:SUBLANES, :], carry_ref[SUBLANES:2 * SUBLANES, :],
                   carry_ref[2 * SUBLANES:3 * SUBLANES, :], carry_ref[3 * SUBLANES:4 * SUBLANES, :]),
                  unroll=SCAN_UNROLL)

    o_ref[...] = hf_ref[N_META:t_all, :] + hb_ref[N_META:t_all, :]


def _lru(a_f, b_f, a_b, b_b, seq):
    batch, t_all, _ = a_f.shape
    n_slab = LRU_WIDTH // LANES
    slab = pl.BlockSpec((None, t_all, LANES), lambda b, s: (b, 0, s))
    return pl.pallas_call(
        _lru_kernel,
        grid=(batch, n_slab),
        in_specs=[slab] * 4,
        out_specs=pl.BlockSpec((None, seq, LANES), lambda b, s: (b, 0, s)),
        out_shape=jax.ShapeDtypeStruct((batch, seq, LRU_WIDTH), f32),
        scratch_shapes=[pltpu.VMEM((t_all, LANES), f32)] * 2
                       + [pltpu.VMEM((4 * N_SEG, LANES), f32), pltpu.VMEM((2 * N_SEG, LANES), f32)],
        compiler_params=pltpu.CompilerParams(dimension_semantics=("arbitrary", "arbitrary"),
                                             vmem_limit_bytes=VMEM_LIMIT),
        name="lru",
    )(a_f, b_f, a_b, b_b)


def _cmul_const(x, c, s):
    re, im = x
    tol = 1e-12
    if abs(s) < tol:
        return (re, im) if c > 0 else (-re, -im)
    if abs(c) < tol:
        return (-im, re) if s > 0 else (im, -re)
    if abs(abs(c) - abs(s)) < tol:
        m = abs(c)
        sc, ss = (1.0 if c > 0 else -1.0), (1.0 if s > 0 else -1.0)
        return (m * (sc * re - ss * im), m * (ss * re + sc * im))
    return (c * re - s * im, s * re + c * im)


def _fft(xs):
    n = len(xs)
    if n == 1:
        return xs
    ev = _fft(xs[0::2])
    od = _fft(xs[1::2])
    out = [None] * n
    for k in range(n // 2):
        ang = -2.0 * math.pi * k / n
        t = _cmul_const(od[k], math.cos(ang), math.sin(ang))
        out[k] = (ev[k][0] + t[0], ev[k][1] + t[1])
        out[k + n // 2] = (ev[k][0] - t[0], ev[k][1] - t[1])
    return out


def _fourier_kernel(xa_ref, xb_ref, ma_ref, mb_ref, fm_ref, twc_ref, tws_ref, o_ref,
                    ha_ref, hb_ref, ta_ref, tb_ref, y_ref):
    tail0 = (N_SLAB - 1) * SLAB - N_META
    for m_ref, x_ref, h_ref, t_ref in ((ma_ref, xa_ref, ha_ref, ta_ref), (mb_ref, xb_ref, hb_ref, tb_ref)):
        h_ref[0:N_META, :] = m_ref[...]
        h_ref[N_META:SLAB_PAD, :] = x_ref[0:SLAB_PAD - N_META, :]
        t_ref[SLAB - 1:SLAB_PAD, :] = jnp.zeros((SLAB_PAD - SLAB + 1, LANES), f32)
        t_ref[0:SLAB - 1, :] = x_ref[tail0:tail0 + SLAB - 1, :]
        t_ref[SLAB - 1:SLAB, :] = x_ref[tail0 + SLAB - 1:tail0 + SLAB, :]

    def stage1(it, carry):
        r = pl.multiple_of(it * SUBLANES, SUBLANES)
        rows = pl.ds(r, SUBLANES)

        def slab_rows(x_ref, h_ref, t_ref, t1):
            if t1 == 0:
                return h_ref[rows, :]
            if t1 == N_SLAB - 1:
                return t_ref[rows, :]
            return x_ref[pl.ds(r + (t1 * SLAB - N_META), SUBLANES), :]

        zs = [(slab_rows(xa_ref, ha_ref, ta_ref, t1), slab_rows(xb_ref, hb_ref, tb_ref, t1))
              for t1 in range(N_SLAB)]
        ys = _fft(zs)
        for k1 in range(N_SLAB):
            yr, yi = ys[k1]
            if k1 > 0:
                cw = twc_ref[k1, rows, :]
                sw = tws_ref[k1, rows, :]
                yr, yi = yr * cw + yi * sw, yi * cw - yr * sw
            y_ref[rows, k1 * LANES:(k1 + 1) * LANES] = yr
            y_ref[pl.ds(r + SLAB_PAD, SUBLANES), k1 * LANES:(k1 + 1) * LANES] = yi
        return carry

    lax.fori_loop(0, SLAB_PAD // SUBLANES, stage1, 0)

    for p in range(N_SLAB // 2):
        yp = y_ref[:, 2 * p * LANES:(2 * p + 2) * LANES].astype(bf16)
        g = jnp.dot(fm_ref[...], yp, preferred_element_type=f32)
        o_ref[pl.ds(2 * p, SLAB - 1, stride=N_SLAB), :] = g[:, 0:LANES]
        o_ref[pl.ds(2 * p + 1, SLAB - 1, stride=N_SLAB), :] = g[:, LANES:2 * LANES]


def _fourier(ab, ab_m, fm, twc, tws, batch, seq):
    n_blk = FOURIER_WIDTH // LANES
    x_spec = lambda off: pl.BlockSpec((None, seq, LANES), lambda b, j: (b, 0, j + off))
    m_spec = lambda off: pl.BlockSpec((N_META, LANES), lambda b, j: (0, j + off))
    return pl.pallas_call(
        _fourier_kernel,
        grid=(batch, n_blk),
        in_specs=[x_spec(0), x_spec(n_blk), m_spec(0), m_spec(n_blk),
                  _resident(fm.shape), _resident(twc.shape), _resident(tws.shape)],
        out_specs=pl.BlockSpec((None, seq, LANES), lambda b, j: (b, 0, j)),
        out_shape=jax.ShapeDtypeStruct((batch, seq, FOURIER_WIDTH), f32),
        scratch_shapes=[pltpu.VMEM((SLAB_PAD, LANES), f32)] * 4
                       + [pltpu.VMEM((2 * SLAB_PAD, N_SLAB * LANES), f32)],
        compiler_params=pltpu.CompilerParams(dimension_semantics=("arbitrary", "arbitrary"),
                                             vmem_limit_bytes=VMEM_LIMIT),
        name="fourier",
    )(ab, ab, ab_m, ab_m, fm, twc, tws)


def _fourier_constants():
    n = N_SLAB * SLAB
    k2 = np.arange(1, SLAB)[:, None]
    t2 = np.arange(SLAB_PAD)[None, :]
    ang = 2.0 * np.pi * ((k2 * t2) % SLAB) / SLAB
    valid = (t2 < SLAB)
    fm = np.concatenate([np.cos(ang) * valid, np.sin(ang) * valid], axis=1)
    k1 = np.arange(N_SLAB)[:, None]
    ang2 = 2.0 * np.pi * ((k1 * t2) % n) / n
    lane_bcast = np.ones((1, 1, LANES))
    twc = (np.cos(ang2) * valid)[:, :, None] * lane_bcast
    tws = (np.sin(ang2) * valid)[:, :, None] * lane_bcast
    return (jnp.asarray(fm, f32), jnp.asarray(twc, f32), jnp.asarray(tws, f32))


def _mix_out_kernel(h1_ref, hs_ref, gate_ref, four_ref, gl_ref, fb_ref, gf_ref, wo_ref,
                    g2_ref, w1g_ref, w1u_ref, w2_ref, gfin_ref, o_ref,
                    wo_s, w1g_s, w1u_s, w2_s, mix_ref, act_ref):
    i = pl.program_id(0)

    @pl.when(i < N_FF)
    def _cast():
        _cast_ffn_chunk(i, w1g_ref, w1u_ref, w2_ref, w1g_s, w1u_s, w2_s)

    @pl.when(i < N_WO)
    def _cast_wo():
        wo_s[i] = wo_ref[...].astype(bf16)

    @pl.when(i >= N_FF)
    def _tile():
        y_l = _rms_norm(hs_ref[...] * _gelu_tanh(gate_ref[...]), gl_ref[...])
        y_f = _rms_norm(four_ref[...] + fb_ref[...], gf_ref[...])
        mix_ref[:, 0:LRU_WIDTH] = y_l.astype(bf16)
        mix_ref[:, LRU_WIDTH:] = y_f.astype(bf16)
        mixed = mix_ref[...]
        proj = jnp.concatenate([jnp.dot(mixed, wo_s[c], preferred_element_type=f32) for c in range(N_WO)], axis=1)
        h2 = h1_ref[...] + proj
        xn = _rms_norm(h2, g2_ref[...]).astype(bf16)
        h3 = h2 + 0.5 * _swiglu(xn, w1g_s, w1u_s, w2_s, act_ref)
        o_ref[...] = _rms_norm(h3, gfin_ref[...])


def _mix_out(h1, hs, gate, four, gl, fb, gf, w_out, g2, w_ffn_in, w_ffn_out, gfin, tile):
    rows = h1.shape[0]
    row_spec = lambda w: pl.BlockSpec((tile, w), lambda i: (jnp.maximum(i - N_FF, 0), 0))
    return pl.pallas_call(
        _mix_out_kernel,
        grid=(N_FF + rows // tile,),
        in_specs=[row_spec(D_MODEL), row_spec(LRU_WIDTH), row_spec(LRU_WIDTH), row_spec(FOURIER_WIDTH),
                  _resident((1, LRU_WIDTH)), _resident((1, FOURIER_WIDTH)), _resident((1, FOURIER_WIDTH)),
                  pl.BlockSpec((D_MODEL, MXU_COLS), lambda i: (0, jnp.minimum(i, N_WO - 1))),
                  _resident((1, D_MODEL)), *_ffn_weight_specs(), _resident((1, D_MODEL))],
        out_specs=row_spec(D_MODEL),
        out_shape=jax.ShapeDtypeStruct((rows, D_MODEL), f32),
        scratch_shapes=[pltpu.VMEM((N_WO, D_MODEL, MXU_COLS), bf16)] + _ffn_weight_scratch()
                       + [pltpu.VMEM((tile, D_MODEL), bf16), pltpu.VMEM((tile, D_FF), bf16)],
        compiler_params=pltpu.CompilerParams(dimension_semantics=("arbitrary",), vmem_limit_bytes=VMEM_LIMIT),
        name="mix_out",
    )(h1, hs, gate, four, gl, fb, gf, w_out, g2, w_ffn_in, w_ffn_in, w_ffn_out, gfin)


def _gate_weights(wa_f, wx_f, wa_b, wx_b, ba_f, bx_f, ba_b, bx_b):
    n_slab = LRU_WIDTH // LANES
    per = LANES // LRU_HEAD_DIM
    eye = jnp.eye(per, dtype=f32)

    def slab_bd(w):
        w = w.astype(f32).reshape(n_slab, per, LRU_HEAD_DIM, LRU_HEAD_DIM)
        return (eye[None, :, None, :, None] * w[:, :, :, None, :]).reshape(n_slab, LANES, LANES)

    wg = jnp.concatenate([slab_bd(w) for w in (wa_f, wx_f, wa_b, wx_b)], axis=-1)
    bg = jnp.concatenate([b.astype(f32).reshape(n_slab, 1, LANES) for b in (ba_f, bx_f, ba_b, bx_b)], axis=-1)
    return (0.5 * wg).astype(bf16), 0.5 * bg


def kernel(x, meta_tokens, norm_ffn1, w_ffn1_in, w_ffn1_out, norm_mix, w_in, conv_w, conv_b, lru_wa_fwd, lru_ba_fwd, lru_wx_fwd, lru_bx_fwd, lru_lambda_fwd, lru_wa_bwd, lru_ba_bwd, lru_wx_bwd, lru_bx_bwd, lru_lambda_bwd, fourier_w, fourier_b, norm_lru_out, norm_fourier_out, w_out, norm_ffn2, w_ffn2_in, w_ffn2_out, norm_final):
    batch, seq, d = x.shape
    assert (d, seq + N_META) == (D_MODEL, N_SLAB * SLAB) and norm_ffn1.shape[0] == 1
    l = 0
    row = lambda v: v.astype(f32).reshape(1, -1)

    mix = _fourier_channel_mix(fourier_w[l]).astype(bf16)
    wg, bg = _gate_weights(lru_wa_fwd[l], lru_wx_fwd[l], lru_wa_bwd[l], lru_wx_bwd[l],
                           lru_ba_fwd[l], lru_bx_fwd[l], lru_ba_bwd[l], lru_bx_bwd[l])
    lam = jnp.stack([lru_lambda_fwd[l], lru_lambda_bwd[l]]).astype(f32)

    h1, gate, ab, ab_m, a_f, b_f, a_b, b_b = _ffn_in(
        x.astype(f32), meta_tokens.astype(f32), row(norm_ffn1[l]),
        w_ffn1_in[l].astype(f32), w_ffn1_out[l].astype(f32), row(norm_mix[l]), w_in[l].astype(f32), mix,
        conv_w[l].astype(f32), row(conv_b[l]), wg, bg, lam)

    hsum = _lru(a_f, b_f, a_b, b_b, seq)

    fm, twc, tws = _fourier_constants()
    four = _fourier(ab, ab_m, fm.astype(bf16), twc, tws, batch, seq)

    out = _mix_out(h1.reshape(batch * seq, d), hsum.reshape(batch * seq, LRU_WIDTH),
                   gate.reshape(batch * seq, LRU_WIDTH), four.reshape(batch * seq, FOURIER_WIDTH),
                   row(norm_lru_out[l]), row(fourier_b[l]), row(norm_fourier_out[l]), w_out[l].astype(f32),
                   row(norm_ffn2[l]), w_ffn2_in[l].astype(f32), w_ffn2_out[l].astype(f32), row(norm_final),
                   tile=ROW_TILE)
    return out.reshape(batch, seq, d).astype(x.dtype)
```

```python
import math

import numpy as np
import jax
import jax.numpy as jnp
from jax import lax
from jax.experimental import pallas as pl
from jax.experimental.pallas import tpu as pltpu

D_MODEL = 1024
N_META = 16
LRU_WIDTH = 512
LRU_HEADS = 8
LRU_HEAD_DIM = 64
FOURIER_WIDTH = 512
FOURIER_GROUPS = 8
FOURIER_GROUP_DIM = 64
D_FF = 2816
EPS = 1e-6
RG_LRU_C = 8.0

LANES = 128
SUBLANES = 8
ROW_TILE = 512
TILES_PER_BATCH = 16
STEPS_PER_BATCH = TILES_PER_BATCH + 2
GATE_PIECES = 8
SEQ_HEAD = 24
MXU_COLS = 256
N_FF = D_FF // MXU_COLS
N_WIN = (2 * LRU_WIDTH + FOURIER_WIDTH) // MXU_COLS
N_WO = D_MODEL // MXU_COLS
VMEM_LIMIT = 60 * 1024 * 1024

N_SLAB = N_META
SLAB = 513
SLAB_PAD = 520
N_SEG = 16
SEG_LEN = 513
SCAN_UNROLL = 9

f32 = jnp.float32
bf16 = jnp.bfloat16


def _rms_norm(x, g):
    return x * lax.rsqrt(jnp.mean(x * x, axis=-1, keepdims=True) + EPS) * g


def _sigmoid(x):
    return 0.5 * (1.0 + jnp.tanh(0.5 * x))


def _gelu_tanh(x):
    c = math.sqrt(2.0 / math.pi)
    return 0.5 * x * (1.0 + jnp.tanh(c * (x + 0.044715 * (x * x * x))))


def _swiglu(xn, w1g_s, w1u_s, w2_s, act_ref, between=None):
    for c in range(N_FF):
        g = jnp.dot(xn, w1g_s[c], preferred_element_type=f32)
        u = jnp.dot(xn, w1u_s[c], preferred_element_type=f32)
        act_ref[:, c * MXU_COLS:(c + 1) * MXU_COLS] = (g * _sigmoid(g) * u).astype(bf16)
        if between is not None:
            between(c)
    return jnp.dot(act_ref[...], w2_s[...], preferred_element_type=f32)


def _cast_ffn_chunk(i, w1g_ref, w1u_ref, w2_ref, w1g_s, w1u_s, w2_s):
    w1g_s[i] = w1g_ref[...].astype(bf16)
    w1u_s[i] = w1u_ref[...].astype(bf16)
    w2_s[pl.ds(pl.multiple_of(i * MXU_COLS, MXU_COLS), MXU_COLS), :] = w2_ref[...].astype(bf16)


def _ffn_weight_specs():
    last = N_FF - 1
    return [pl.BlockSpec((D_MODEL, MXU_COLS), lambda i: (0, jnp.minimum(i, last))),
            pl.BlockSpec((D_MODEL, MXU_COLS), lambda i: (0, N_FF + jnp.minimum(i, last))),
            pl.BlockSpec((MXU_COLS, D_MODEL), lambda i: (jnp.minimum(i, last), 0))]


def _ffn_weight_scratch():
    return [pltpu.VMEM((N_FF, D_MODEL, MXU_COLS), bf16), pltpu.VMEM((N_FF, D_MODEL, MXU_COLS), bf16),
            pltpu.VMEM((D_FF, D_MODEL), bf16)]


def _resident(shape):
    nd = len(shape)
    return pl.BlockSpec(shape, lambda *_: (0,) * nd, pipeline_mode=pl.Buffered(1))


def _chanmix_kernel(bd_ref, cd_ref, sd_ref, o_ref):
    hi = lax.Precision.HIGHEST
    for h in range(FOURIER_WIDTH // MXU_COLS):
        blk = slice(h * MXU_COLS, (h + 1) * MXU_COLS)
        bd = bd_ref[blk, blk]
        o_ref[h, :, 0:MXU_COLS] = jnp.dot(cd_ref[blk, blk], bd, preferred_element_type=f32, precision=hi)
        o_ref[h, :, MXU_COLS:] = jnp.dot(sd_ref[blk, blk], bd, preferred_element_type=f32, precision=hi)


def _fourier_channel_mix(fourier_w):
    g, n = FOURIER_GROUPS, FOURIER_GROUP_DIM
    eye = np.eye(g, dtype=np.float64)
    ang = 2.0 * np.pi * (np.outer(np.arange(n), np.arange(n)) % n) / n
    scale = 1.0 / math.sqrt((N_SLAB * SLAB) * n)
    cd = jnp.asarray(np.kron(eye, np.cos(ang)) * scale, f32)
    sd = jnp.asarray(np.kron(eye, -np.sin(ang)) * scale, f32)
    bd = (jnp.eye(g, dtype=f32)[:, None, :, None] * fourier_w.astype(f32)[:, :, None, :]).reshape(g * n, g * n)
    return pl.pallas_call(
        _chanmix_kernel,
        out_shape=jax.ShapeDtypeStruct((FOURIER_WIDTH // MXU_COLS, MXU_COLS, 2 * MXU_COLS), f32),
        name="fourier_channel_mix",
    )(bd, cd, sd)


def _lru_gate_rows(seq_s, r0, n, cw, cb, wg_ref, bg_ref, k_f, k_b, outs, slabs=range(LRU_WIDTH // LANES)):
    af_ref, bf_ref, ab_ref, bb_ref = outs

    def gates(half_pre_r, half_pre_i, x, k_dir):
        t = jnp.tanh(k_dir * jnp.tanh(half_pre_r) + k_dir)
        q = 1.0 / (1.0 - t)
        a = (1.0 + t) * q
        y = -t
        root = jnp.where(y > 0.0, y * lax.rsqrt(y), 0.0)
        return a, (root * q) * ((1.0 + jnp.tanh(half_pre_i)) * x)

    for s in slabs:
        lanes = slice(s * LANES, (s + 1) * LANES)
        x = (cw[0:1, lanes] * seq_s[r0 + 6:r0 + 6 + n, lanes] + cw[1:2, lanes] * seq_s[r0 + 7:r0 + 7 + n, lanes]
             + cw[2:3, lanes] * seq_s[r0 + 8:r0 + 8 + n, lanes] + cw[3:4, lanes] * seq_s[r0 + 9:r0 + 9 + n, lanes]
             + cb[:, lanes])
        pre = jnp.dot(x.astype(bf16), wg_ref[s], preferred_element_type=f32) + bg_ref[s]
        a_f, b_f = gates(pre[:, 0:LANES], pre[:, LANES:2 * LANES], x, k_f[:, lanes])
        a_b, b_b = gates(pre[:, 2 * LANES:3 * LANES], pre[:, 3 * LANES:4 * LANES], x, k_b[:, lanes])
        af_ref[r0:r0 + n, lanes] = a_f
        bf_ref[r0:r0 + n, lanes] = b_f
        ab_ref[r0:r0 + n, lanes] = a_b
        bb_ref[r0:r0 + n, lanes] = b_b


def _ffn_in_kernel(x_ref, xm_ref, g1_ref, w1g_ref, w1u_ref, w2_ref, g2_ref, win_ref, mix_ref,
                   cw_ref, cb_ref, wg_ref, bg_ref, lam_ref,
                   h1_ref, gate_ref, ab_ref, abm_ref, af_ref, bf_ref, abk_ref, bb_ref,
                   w1g_s, w1u_s, w2_s, win_s, act_ref, actm_ref, seq_s, carry_s, lrum_s):
    i = pl.program_id(0)
    q = jnp.maximum(i - (N_FF + 1), 0)
    j = q % STEPS_PER_BATCH
    in_batch = i > N_FF
    gate_outs = (af_ref, bf_ref, abk_ref, bb_ref)

    @pl.when(i < N_FF)
    def _cast():
        _cast_ffn_chunk(i, w1g_ref, w1u_ref, w2_ref, w1g_s, w1u_s, w2_s)

    @pl.when(i < N_WIN)
    def _cast_win():
        win_s[i] = win_ref[...].astype(bf16)

    @pl.when(i == 0)
    def _init():
        seq_s[...] = jnp.zeros(seq_s.shape, f32)
        carry_s[...] = jnp.zeros(carry_s.shape, f32)

    def rows_body(x, act, between=None):
        xn = _rms_norm(x, g1_ref[...]).astype(bf16)
        h1 = x + 0.5 * _swiglu(xn, w1g_s, w1u_s, w2_s, act, between)
        xn2 = _rms_norm(h1, g2_ref[...]).astype(bf16)
        u = [jnp.dot(xn2, win_s[c], preferred_element_type=f32) for c in range(N_WIN)]
        n_lru = LRU_WIDTH // MXU_COLS
        lru = jnp.concatenate(u[0:n_lru], axis=1)
        gate = jnp.concatenate(u[n_lru:2 * n_lru], axis=1)
        ab = [jnp.dot(v.astype(bf16), mix_ref[h], preferred_element_type=f32)
              for h, v in enumerate(u[2 * n_lru:])]
        a_b = jnp.concatenate([p[:, 0:MXU_COLS] for p in ab] + [p[:, MXU_COLS:] for p in ab], axis=1)
        return h1, lru, gate, a_b

    def gate_params():
        nlam = -lam_ref[...]
        sp = jnp.maximum(nlam, 0.0) + jnp.log1p(jnp.exp(-jnp.abs(nlam)))
        return cw_ref[...], cb_ref[...], (-0.25 * RG_LRU_C) * sp[0:1, :], (-0.25 * RG_LRU_C) * sp[1:2, :]

    @pl.when(i == N_FF)
    def _meta():
        _, lru, _, a_b = rows_body(xm_ref[...], actm_ref)
        lrum_s[...] = lru
        abm_ref[...] = a_b

    @pl.when(in_batch & (j < TILES_PER_BATCH))
    def _tile():
        cw, cb, k_f, k_b = gate_params()
        n_slab = LRU_WIDTH // LANES
        piece = ROW_TILE * n_slab // GATE_PIECES

        def between(c):
            if c < GATE_PIECES:
                slab, part = c % n_slab, c // n_slab
                _lru_gate_rows(seq_s, part * piece, piece, cw, cb, wg_ref, bg_ref, k_f, k_b, gate_outs,
                               slabs=(slab,))

        h1, lru, gate, a_b = rows_body(x_ref[...], act_ref, between)
        h1_ref[...] = h1
        gate_ref[...] = _gelu_tanh(gate)
        ab_ref[...] = a_b
        head = jnp.concatenate([jnp.zeros((SUBLANES, LRU_WIDTH), f32), lrum_s[...]], axis=0)
        seq_s[0:SEQ_HEAD, :] = jnp.where(j == 0, head, carry_s[...])
        seq_s[SEQ_HEAD:SEQ_HEAD + ROW_TILE, :] = lru
        carry_s[...] = lru[ROW_TILE - SEQ_HEAD:ROW_TILE, :]

    @pl.when(in_batch & (j == TILES_PER_BATCH))
    def _last_block():
        cw, cb, k_f, k_b = gate_params()
        _lru_gate_rows(seq_s, 0, ROW_TILE, cw, cb, wg_ref, bg_ref, k_f, k_b, gate_outs)
        seq_s[0:SEQ_HEAD, :] = carry_s[...]
        seq_s[SEQ_HEAD:SEQ_HEAD + SUBLANES, :] = jnp.zeros((SUBLANES, LRU_WIDTH), f32)

    @pl.when(in_batch & (j == TILES_PER_BATCH + 1))
    def _tail_block():
        cw, cb, k_f, k_b = gate_params()
        _lru_gate_rows(seq_s, 0, N_META, cw, cb, wg_ref, bg_ref, k_f, k_b, gate_outs)


def _ffn_in(x3d, xm, g1, w_ffn_in, w_ffn_out, g2, w_in, mix, conv_w, conv_b, wg, bg, lam):
    batch, seq, _ = x3d.shape
    assert seq == TILES_PER_BATCH * ROW_TILE
    n_meta = xm.shape[0]
    lead = N_FF + 1

    def tile_index(i):
        q = jnp.maximum(i - lead, 0)
        return q // STEPS_PER_BATCH, jnp.minimum(q % STEPS_PER_BATCH, TILES_PER_BATCH - 1)

    def block_index(i):
        q = jnp.maximum(i - lead, 0)
        return q // STEPS_PER_BATCH, jnp.clip(q % STEPS_PER_BATCH - 1, 0, TILES_PER_BATCH)

    row_spec = lambda w: pl.BlockSpec((None, ROW_TILE, w), lambda i: (*tile_index(i), 0))
    seq_spec = pl.BlockSpec((None, ROW_TILE, LRU_WIDTH), lambda i: (*block_index(i), 0))
    whole = lambda shape: pl.BlockSpec(shape, lambda i: (0,) * len(shape))
    coeff = jax.ShapeDtypeStruct((batch, n_meta + seq, LRU_WIDTH), f32)
    return pl.pallas_call(
        _ffn_in_kernel,
        grid=(lead + batch * STEPS_PER_BATCH,),
        in_specs=[row_spec(D_MODEL), _resident(xm.shape), _resident((1, D_MODEL)), *_ffn_weight_specs(),
                  _resident((1, D_MODEL)),
                  pl.BlockSpec((D_MODEL, MXU_COLS), lambda i: (0, jnp.minimum(i, N_WIN - 1))),
                  _resident(mix.shape), _resident(conv_w.shape), _resident(conv_b.shape),
                  _resident(wg.shape), _resident(bg.shape), _resident(lam.shape)],
        out_specs=[row_spec(D_MODEL), row_spec(LRU_WIDTH), row_spec(2 * FOURIER_WIDTH),
                   whole((n_meta, 2 * FOURIER_WIDTH)), seq_spec, seq_spec, seq_spec, seq_spec],
        out_shape=[jax.ShapeDtypeStruct((batch, seq, D_MODEL), f32),
                   jax.ShapeDtypeStruct((batch, seq, LRU_WIDTH), f32),
                   jax.ShapeDtypeStruct((batch, seq, 2 * FOURIER_WIDTH), f32),
                   jax.ShapeDtypeStruct((n_meta, 2 * FOURIER_WIDTH), f32),
                   coeff, coeff, coeff, coeff],
        scratch_shapes=_ffn_weight_scratch()
                       + [pltpu.VMEM((N_WIN, D_MODEL, MXU_COLS), bf16),
                          pltpu.VMEM((ROW_TILE, D_FF), bf16), pltpu.VMEM((n_meta, D_FF), bf16),
                          pltpu.VMEM((SEQ_HEAD + ROW_TILE, LRU_WIDTH), f32),
                          pltpu.VMEM((SEQ_HEAD, LRU_WIDTH), f32), pltpu.VMEM((n_meta, LRU_WIDTH), f32)],
        compiler_params=pltpu.CompilerParams(dimension_semantics=("arbitrary",), vmem_limit_bytes=VMEM_LIMIT),
        name="ffn_in",
    )(x3d, xm, g1, w_ffn_in, w_ffn_in, w_ffn_out, g2, w_in, mix, conv_w, conv_b, wg, bg, lam)


def _lru_kernel(af_ref, bf_ref, ab_ref, bb_ref, o_ref, hf_ref, hb_ref, ends_ref, carry_ref):
    t_all = N_SEG * SEG_LEN
    zero8 = jnp.zeros((SUBLANES, LANES), f32)
    half = SUBLANES * SEG_LEN

    def seg_rows(j, hi):
        return pl.ds(j + hi * half, SUBLANES, stride=SEG_LEN)

    def pass1(j, st):
        hf0, hf1, pf0, pf1, hb0, hb1, pb0, pb1 = st
        jb = SEG_LEN - 1 - j
        a0 = af_ref[seg_rows(j, 0), :]
        a1 = af_ref[seg_rows(j, 1), :]
        hf0 = a0 * hf0 + bf_ref[seg_rows(j, 0), :]
        hf1 = a1 * hf1 + bf_ref[seg_rows(j, 1), :]
        pf0 = a0 * pf0
        pf1 = a1 * pf1
        c0 = ab_ref[seg_rows(jb, 0), :]
        c1 = ab_ref[seg_rows(jb, 1), :]
        hb0 = c0 * hb0 + bb_ref[seg_rows(jb, 0), :]
        hb1 = c1 * hb1 + bb_ref[seg_rows(jb, 1), :]
        pb0 = c0 * pb0
        pb1 = c1 * pb1
        return hf0, hf1, pf0, pf1, hb0, hb1, pb0, pb1

    one8 = jnp.ones((SUBLANES, LANES), f32)
    st = lax.fori_loop(0, SEG_LEN, pass1, (zero8, zero8, one8, one8, zero8, zero8, one8, one8),
                       unroll=SCAN_UNROLL)
    for idx, v in enumerate(st):
        ends_ref[idx * SUBLANES:(idx + 1) * SUBLANES, :] = v

    c = jnp.zeros((1, LANES), f32)
    for s in range(N_SEG):
        carry_ref[s:s + 1, :] = c
        c = ends_ref[s:s + 1, :] + ends_ref[N_SEG + s:N_SEG + s + 1, :] * c
    c = jnp.zeros((1, LANES), f32)
    for s in range(N_SEG - 1, -1, -1):
        carry_ref[N_SEG + s:N_SEG + s + 1, :] = c
        c = ends_ref[2 * N_SEG + s:2 * N_SEG + s + 1, :] + ends_ref[3 * N_SEG + s:3 * N_SEG + s + 1, :] * c

    def pass2(j, st):
        hf0, hf1, hb0, hb1 = st
        jb = SEG_LEN - 1 - j
        hf0 = af_ref[seg_rows(j, 0), :] * hf0 + bf_ref[seg_rows(j, 0), :]
        hf1 = af_ref[seg_rows(j, 1), :] * hf1 + bf_ref[seg_rows(j, 1), :]
        hf_ref[seg_rows(j, 0), :] = hf0
        hf_ref[seg_rows(j, 1), :] = hf1
        hb0 = ab_ref[seg_rows(jb, 0), :] * hb0 + bb_ref[seg_rows(jb, 0), :]
        hb1 = ab_ref[seg_rows(jb, 1), :] * hb1 + bb_ref[seg_rows(jb, 1), :]
        hb_ref[seg_rows(jb, 0), :] = hb0
        hb_ref[seg_rows(jb, 1), :] = hb1
        return hf0, hf1, hb0, hb1

    lax.fori_loop(0, SEG_LEN, pass2,
                  (carry_ref[0:SUBLANES, :], carry_ref[SUBLANES:2 * SUBLANES, :],
                   carry_ref[2 * SUBLANES:3 * SUBLANES, :], carry_ref[3 * SUBLANES:4 * SUBLANES, :]),
                  unroll=SCAN_UNROLL)

    o_ref[...] = hf_ref[N_META:t_all, :] + hb_ref[N_META:t_all, :]


def _lru(a_f, b_f, a_b, b_b, seq):
    batch, t_all, _ = a_f.shape
    n_slab = LRU_WIDTH // LANES
    slab = pl.BlockSpec((None, t_all, LANES), lambda b, s: (b, 0, s))
    return pl.pallas_call(
        _lru_kernel,
        grid=(batch, n_slab),
        in_specs=[slab] * 4,
        out_specs=pl.BlockSpec((None, seq, LANES), lambda b, s: (b, 0, s)),
        out_shape=jax.ShapeDtypeStruct((batch, seq, LRU_WIDTH), f32),
        scratch_shapes=[pltpu.VMEM((t_all, LANES), f32)] * 2
                       + [pltpu.VMEM((4 * N_SEG, LANES), f32), pltpu.VMEM((2 * N_SEG, LANES), f32)],
        compiler_params=pltpu.CompilerParams(dimension_semantics=("arbitrary", "arbitrary"),
                                             vmem_limit_bytes=VMEM_LIMIT),
        name="lru",
    )(a_f, b_f, a_b, b_b)


def _cmul_const(x, c, s):
    re, im = x
    tol = 1e-12
    if abs(s) < tol:
        return (re, im) if c > 0 else (-re, -im)
    if abs(c) < tol:
        return (-im, re) if s > 0 else (im, -re)
    if abs(abs(c) - abs(s)) < tol:
        m = abs(c)
        sc, ss = (1.0 if c > 0 else -1.0), (1.0 if s > 0 else -1.0)
        return (m * (sc * re - ss * im), m * (ss * re + sc * im))
    return (c * re - s * im, s * re + c * im)


def _fft(xs):
    n = len(xs)
    if n == 1:
        return xs
    ev = _fft(xs[0::2])
    od = _fft(xs[1::2])
    out = [None] * n
    for k in range(n // 2):
        ang = -2.0 * math.pi * k / n
        t = _cmul_const(od[k], math.cos(ang), math.sin(ang))
        out[k] = (ev[k][0] + t[0], ev[k][1] + t[1])
        out[k + n // 2] = (ev[k][0] - t[0], ev[k][1] - t[1])
    return out


def _fourier_kernel(xa_ref, xb_ref, ma_ref, mb_ref, fm_ref, twc_ref, tws_ref, o_ref,
                    ha_ref, hb_ref, ta_ref, tb_ref, y_ref):
    tail0 = (N_SLAB - 1) * SLAB - N_META
    for m_ref, x_ref, h_ref, t_ref in ((ma_ref, xa_ref, ha_ref, ta_ref), (mb_ref, xb_ref, hb_ref, tb_ref)):
        h_ref[0:N_META, :] = m_ref[...]
        h_ref[N_META:SLAB_PAD, :] = x_ref[0:SLAB_PAD - N_META, :]
        t_ref[SLAB - 1:SLAB_PAD, :] = jnp.zeros((SLAB_PAD - SLAB + 1, LANES), f32)
        t_ref[0:SLAB - 1, :] = x_ref[tail0:tail0 + SLAB - 1, :]
        t_ref[SLAB - 1:SLAB, :] = x_ref[tail0 + SLAB - 1:tail0 + SLAB, :]

    def stage1(it, carry):
        r = pl.multiple_of(it * SUBLANES, SUBLANES)
        rows = pl.ds(r, SUBLANES)

        def slab_rows(x_ref, h_ref, t_ref, t1):
            if t1 == 0:
                return h_ref[rows, :]
            if t1 == N_SLAB - 1:
                return t_ref[rows, :]
            return x_ref[pl.ds(r + (t1 * SLAB - N_META), SUBLANES), :]

        zs = [(slab_rows(xa_ref, ha_ref, ta_ref, t1), slab_rows(xb_ref, hb_ref, tb_ref, t1))
              for t1 in range(N_SLAB)]
        ys = _fft(zs)
        for k1 in range(N_SLAB):
            yr, yi = ys[k1]
            if k1 > 0:
                cw = twc_ref[k1, rows, :]
                sw = tws_ref[k1, rows, :]
                yr, yi = yr * cw + yi * sw, yi * cw - yr * sw
            y_ref[rows, k1 * LANES:(k1 + 1) * LANES] = yr
            y_ref[pl.ds(r + SLAB_PAD, SUBLANES), k1 * LANES:(k1 + 1) * LANES] = yi
        return carry

    lax.fori_loop(0, SLAB_PAD // SUBLANES, stage1, 0)

    for p in range(N_SLAB // 2):
        yp = y_ref[:, 2 * p * LANES:(2 * p + 2) * LANES].astype(bf16)
        g = jnp.dot(fm_ref[...], yp, preferred_element_type=f32)
        o_ref[pl.ds(2 * p, SLAB - 1, stride=N_SLAB), :] = g[:, 0:LANES]
        o_ref[pl.ds(2 * p + 1, SLAB - 1, stride=N_SLAB), :] = g[:, LANES:2 * LANES]


def _fourier(ab, ab_m, fm, twc, tws, batch, seq):
    n_blk = FOURIER_WIDTH // LANES
    x_spec = lambda off: pl.BlockSpec((None, seq, LANES), lambda b, j: (b, 0, j + off))
    m_spec = lambda off: pl.BlockSpec((N_META, LANES), lambda b, j: (0, j + off))
    return pl.pallas_call(
        _fourier_kernel,
        grid=(batch, n_blk),
        in_specs=[x_spec(0), x_spec(n_blk), m_spec(0), m_spec(n_blk),
                  _resident(fm.shape), _resident(twc.shape), _resident(tws.shape)],
        out_specs=pl.BlockSpec((None, seq, LANES), lambda b, j: (b, 0, j)),
        out_shape=jax.ShapeDtypeStruct((batch, seq, FOURIER_WIDTH), f32),
        scratch_shapes=[pltpu.VMEM((SLAB_PAD, LANES), f32)] * 4
                       + [pltpu.VMEM((2 * SLAB_PAD, N_SLAB * LANES), f32)],
        compiler_params=pltpu.CompilerParams(dimension_semantics=("arbitrary", "arbitrary"),
                                             vmem_limit_bytes=VMEM_LIMIT),
        name="fourier",
    )(ab, ab, ab_m, ab_m, fm, twc, tws)


def _fourier_constants():
    n = N_SLAB * SLAB
    k2 = np.arange(1, SLAB)[:, None]
    t2 = np.arange(SLAB_PAD)[None, :]
    ang = 2.0 * np.pi * ((k2 * t2) % SLAB) / SLAB
    valid = (t2 < SLAB)
    fm = np.concatenate([np.cos(ang) * valid, np.sin(ang) * valid], axis=1)
    k1 = np.arange(N_SLAB)[:, None]
    ang2 = 2.0 * np.pi * ((k1 * t2) % n) / n
    lane_bcast = np.ones((1, 1, LANES))
    twc = (np.cos(ang2) * valid)[:, :, None] * lane_bcast
    tws = (np.sin(ang2) * valid)[:, :, None] * lane_bcast
    return (jnp.asarray(fm, f32), jnp.asarray(twc, f32), jnp.asarray(tws, f32))


def _mix_out_kernel(h1_ref, hs_ref, gate_ref, four_ref, gl_ref, fb_ref, gf_ref, wo_ref,
                    g2_ref, w1g_ref, w1u_ref, w2_ref, gfin_ref, o_ref,
                    wo_s, w1g_s, w1u_s, w2_s, mix_ref, act_ref):
    i = pl.program_id(0)

    @pl.when(i < N_FF)
    def _cast():
        _cast_ffn_chunk(i, w1g_ref, w1u_ref, w2_ref, w1g_s, w1u_s, w2_s)

    @pl.when(i < N_WO)
    def _cast_wo():
        wo_s[i] = wo_ref[...].astype(bf16)

    @pl.when(i >= N_FF)
    def _tile():
        y_l = _rms_norm(hs_ref[...] * gate_ref[...], gl_ref[...])
        y_f = _rms_norm(four_ref[...] + fb_ref[...], gf_ref[...])
        mix_ref[:, 0:LRU_WIDTH] = y_l.astype(bf16)
        mix_ref[:, LRU_WIDTH:] = y_f.astype(bf16)
        mixed = mix_ref[...]
        proj = jnp.concatenate([jnp.dot(mixed, wo_s[c], preferred_element_type=f32) for c in range(N_WO)], axis=1)
        h2 = h1_ref[...] + proj
        xn = _rms_norm(h2, g2_ref[...]).astype(bf16)
        h3 = h2 + 0.5 * _swiglu(xn, w1g_s, w1u_s, w2_s, act_ref)
        o_ref[...] = _rms_norm(h3, gfin_ref[...])


def _mix_out(h1, hs, gate, four, gl, fb, gf, w_out, g2, w_ffn_in, w_ffn_out, gfin, tile):
    rows = h1.shape[0]
    row_spec = lambda w: pl.BlockSpec((tile, w), lambda i: (jnp.maximum(i - N_FF, 0), 0))
    return pl.pallas_call(
        _mix_out_kernel,
        grid=(N_FF + rows // tile,),
        in_specs=[row_spec(D_MODEL), row_spec(LRU_WIDTH), row_spec(LRU_WIDTH), row_spec(FOURIER_WIDTH),
                  _resident((1, LRU_WIDTH)), _resident((1, FOURIER_WIDTH)), _resident((1, FOURIER_WIDTH)),
                  pl.BlockSpec((D_MODEL, MXU_COLS), lambda i: (0, jnp.minimum(i, N_WO - 1))),
                  _resident((1, D_MODEL)), *_ffn_weight_specs(), _resident((1, D_MODEL))],
        out_specs=row_spec(D_MODEL),
        out_shape=jax.ShapeDtypeStruct((rows, D_MODEL), f32),
        scratch_shapes=[pltpu.VMEM((N_WO, D_MODEL, MXU_COLS), bf16)] + _ffn_weight_scratch()
                       + [pltpu.VMEM((tile, D_MODEL), bf16), pltpu.VMEM((tile, D_FF), bf16)],
        compiler_params=pltpu.CompilerParams(dimension_semantics=("arbitrary",), vmem_limit_bytes=VMEM_LIMIT),
        name="mix_out",
    )(h1, hs, gate, four, gl, fb, gf, w_out, g2, w_ffn_in, w_ffn_in, w_ffn_out, gfin)


def _gate_weights(wa_f, wx_f, wa_b, wx_b, ba_f, bx_f, ba_b, bx_b):
    n_slab = LRU_WIDTH // LANES
    per = LANES // LRU_HEAD_DIM
    eye = jnp.eye(per, dtype=f32)

    def slab_bd(w):
        w = w.astype(f32).reshape(n_slab, per, LRU_HEAD_DIM, LRU_HEAD_DIM)
        return (eye[None, :, None, :, None] * w[:, :, :, None, :]).reshape(n_slab, LANES, LANES)

    wg = jnp.concatenate([slab_bd(w) for w in (wa_f, wx_f, wa_b, wx_b)], axis=-1)
    bg = jnp.concatenate([b.astype(f32).reshape(n_slab, 1, LANES) for b in (ba_f, bx_f, ba_b, bx_b)], axis=-1)
    return (0.5 * wg).astype(bf16), 0.5 * bg


def kernel(x, meta_tokens, norm_ffn1, w_ffn1_in, w_ffn1_out, norm_mix, w_in, conv_w, conv_b, lru_wa_fwd, lru_ba_fwd, lru_wx_fwd, lru_bx_fwd, lru_lambda_fwd, lru_wa_bwd, lru_ba_bwd, lru_wx_bwd, lru_bx_bwd, lru_lambda_bwd, fourier_w, fourier_b, norm_lru_out, norm_fourier_out, w_out, norm_ffn2, w_ffn2_in, w_ffn2_out, norm_final):
    batch, seq, d = x.shape
    assert (d, seq + N_META) == (D_MODEL, N_SLAB * SLAB) and norm_ffn1.shape[0] == 1
    l = 0
    row = lambda v: v.astype(f32).reshape(1, -1)

    mix = _fourier_channel_mix(fourier_w[l]).astype(bf16)
    wg, bg = _gate_weights(lru_wa_fwd[l], lru_wx_fwd[l], lru_wa_bwd[l], lru_wx_bwd[l],
                           lru_ba_fwd[l], lru_bx_fwd[l], lru_ba_bwd[l], lru_bx_bwd[l])
    lam = jnp.stack([lru_lambda_fwd[l], lru_lambda_bwd[l]]).astype(f32)

    h1, gate, ab, ab_m, a_f, b_f, a_b, b_b = _ffn_in(
        x.astype(f32), meta_tokens.astype(f32), row(norm_ffn1[l]),
        w_ffn1_in[l].astype(f32), w_ffn1_out[l].astype(f32), row(norm_mix[l]), w_in[l].astype(f32), mix,
        conv_w[l].astype(f32), row(conv_b[l]), wg, bg, lam)

    hsum = _lru(a_f, b_f, a_b, b_b, seq)

    fm, twc, tws = _fourier_constants()
    four = _fourier(ab, ab_m, fm.astype(bf16), twc, tws, batch, seq)

    out = _mix_out(h1.reshape(batch * seq, d), hsum.reshape(batch * seq, LRU_WIDTH),
                   gate.reshape(batch * seq, LRU_WIDTH), four.reshape(batch * seq, FOURIER_WIDTH),
                   row(norm_lru_out[l]), row(fourier_b[l]), row(norm_fourier_out[l]), w_out[l].astype(f32),
                   row(norm_ffn2[l]), w_ffn2_in[l].astype(f32), w_ffn2_out[l].astype(f32), row(norm_final),
                   tile=ROW_TILE)
    return out.reshape(batch, seq, d).astype(x.dtype)
```

```python
import math

import numpy as np
import jax
import jax.numpy as jnp
from jax import lax
from jax.experimental import pallas as pl
from jax.experimental.pallas import tpu as pltpu

D_MODEL = 1024
N_META = 16
LRU_WIDTH = 512
LRU_HEAD_DIM = 64
FOURIER_WIDTH = 512
FOURIER_GROUPS = 8
FOURIER_GROUP_DIM = 64
D_FF = 2816
EPS = 1e-6
RG_LRU_C = 8.0

LANES = 128
SUBLANES = 8
ROW_TILE = 512
TILES_PER_BATCH = 16
STEPS_PER_BATCH = TILES_PER_BATCH + 2
GATE_PIECES = 8
SEQ_HEAD = 24
MXU_COLS = 256
N_FF = D_FF // MXU_COLS
N_WIN = (2 * LRU_WIDTH + FOURIER_WIDTH) // MXU_COLS
N_WO = D_MODEL // MXU_COLS
VMEM_LIMIT = 60 * 1024 * 1024

N_SLAB = N_META
SLAB = 513
SLAB_PAD = 520
N_SEG = 16
SEG_LEN = 513
SCAN_UNROLL = 9

f32 = jnp.float32
bf16 = jnp.bfloat16


def _rms_norm(x, g):
    return x * lax.rsqrt(jnp.mean(x * x, axis=-1, keepdims=True) + EPS) * g


def _sigmoid(x):
    return 0.5 * (1.0 + jnp.tanh(0.5 * x))


def _gelu_tanh(x):
    c = math.sqrt(2.0 / math.pi)
    return 0.5 * x * (1.0 + jnp.tanh(c * (x + 0.044715 * (x * x * x))))


def _swiglu(xn, w1g_s, w1u_s, w2_s, act_ref, between=None):
    for c in range(N_FF):
        g = jnp.dot(xn, w1g_s[c], preferred_element_type=f32)
        u = jnp.dot(xn, w1u_s[c], preferred_element_type=f32)
        act_ref[:, c * MXU_COLS:(c + 1) * MXU_COLS] = (g * _sigmoid(g) * u).astype(bf16)
        if between is not None:
            between(c)
    return jnp.dot(act_ref[...], w2_s[...], preferred_element_type=f32)


def _cast_ffn_chunk(i, w1g_ref, w1u_ref, w2_ref, w1g_s, w1u_s, w2_s):
    w1g_s[i] = w1g_ref[...].astype(bf16)
    w1u_s[i] = w1u_ref[...].astype(bf16)
    w2_s[pl.ds(pl.multiple_of(i * MXU_COLS, MXU_COLS), MXU_COLS), :] = w2_ref[...].astype(bf16)


def _ffn_weight_specs():
    last = N_FF - 1
    return [pl.BlockSpec((D_MODEL, MXU_COLS), lambda i: (0, jnp.minimum(i, last))),
            pl.BlockSpec((D_MODEL, MXU_COLS), lambda i: (0, N_FF + jnp.minimum(i, last))),
            pl.BlockSpec((MXU_COLS, D_MODEL), lambda i: (jnp.minimum(i, last), 0))]


def _ffn_weight_scratch():
    return [pltpu.VMEM((N_FF, D_MODEL, MXU_COLS), bf16), pltpu.VMEM((N_FF, D_MODEL, MXU_COLS), bf16),
            pltpu.VMEM((D_FF, D_MODEL), bf16)]


def _resident(shape):
    nd = len(shape)
    return pl.BlockSpec(shape, lambda *_: (0,) * nd, pipeline_mode=pl.Buffered(1))


def _chanmix_kernel(bd_ref, cd_ref, sd_ref, o_ref):
    hi = lax.Precision.HIGHEST
    for h in range(FOURIER_WIDTH // MXU_COLS):
        blk = slice(h * MXU_COLS, (h + 1) * MXU_COLS)
        bd = bd_ref[blk, blk]
        o_ref[h, :, 0:MXU_COLS] = jnp.dot(cd_ref[blk, blk], bd, preferred_element_type=f32, precision=hi)
        o_ref[h, :, MXU_COLS:] = jnp.dot(sd_ref[blk, blk], bd, preferred_element_type=f32, precision=hi)


def _fourier_channel_mix(fourier_w):
    g, n = FOURIER_GROUPS, FOURIER_GROUP_DIM
    eye = np.eye(g, dtype=np.float64)
    ang = 2.0 * np.pi * (np.outer(np.arange(n), np.arange(n)) % n) / n
    scale = 1.0 / math.sqrt((N_SLAB * SLAB) * n)
    cd = jnp.asarray(np.kron(eye, np.cos(ang)) * scale, f32)
    sd = jnp.asarray(np.kron(eye, -np.sin(ang)) * scale, f32)
    bd = (jnp.eye(g, dtype=f32)[:, None, :, None] * fourier_w.astype(f32)[:, :, None, :]).reshape(g * n, g * n)
    return pl.pallas_call(
        _chanmix_kernel,
        out_shape=jax.ShapeDtypeStruct((FOURIER_WIDTH // MXU_COLS, MXU_COLS, 2 * MXU_COLS), f32),
        name="fourier_channel_mix",
    )(bd, cd, sd)


def _lru_gate_rows(seq_s, r0, n, cw, cb, wg_ref, bg_ref, k_f, k_b, outs, slabs=range(LRU_WIDTH // LANES)):
    af_ref, bf_ref, ab_ref, bb_ref = outs

    def gates(half_pre_r, half_pre_i, x, k_dir):
        t = jnp.tanh(k_dir * jnp.tanh(half_pre_r) + k_dir)
        q = 1.0 / (1.0 - t)
        a = (1.0 + t) * q
        y = -t
        root = jnp.where(y > 0.0, y * lax.rsqrt(y), 0.0)
        return a, (root * q) * ((1.0 + jnp.tanh(half_pre_i)) * x)

    for s in slabs:
        lanes = slice(s * LANES, (s + 1) * LANES)
        x = (cw[0:1, lanes] * seq_s[r0 + 6:r0 + 6 + n, lanes] + cw[1:2, lanes] * seq_s[r0 + 7:r0 + 7 + n, lanes]
             + cw[2:3, lanes] * seq_s[r0 + 8:r0 + 8 + n, lanes] + cw[3:4, lanes] * seq_s[r0 + 9:r0 + 9 + n, lanes]
             + cb[:, lanes])
        pre = jnp.dot(x.astype(bf16), wg_ref[s], preferred_element_type=f32) + bg_ref[s]
        a_f, b_f = gates(pre[:, 0:LANES], pre[:, LANES:2 * LANES], x, k_f[:, lanes])
        a_b, b_b = gates(pre[:, 2 * LANES:3 * LANES], pre[:, 3 * LANES:4 * LANES], x, k_b[:, lanes])
        af_ref[s, r0:r0 + n, :] = a_f
        bf_ref[s, r0:r0 + n, :] = b_f
        ab_ref[s, r0:r0 + n, :] = a_b
        bb_ref[s, r0:r0 + n, :] = b_b


def _ffn_in_kernel(x_ref, xm_ref, g1_ref, w1g_ref, w1u_ref, w2_ref, g2_ref, win_ref, mix_ref,
                   cw_ref, cb_ref, wg_ref, bg_ref, lam_ref,
                   h1_ref, gate_ref, ab_ref, abm_ref, af_ref, bf_ref, abk_ref, bb_ref,
                   w1g_s, w1u_s, w2_s, win_s, act_ref, actm_ref, seq_s, carry_s, lrum_s):
    i = pl.program_id(0)
    q = jnp.maximum(i - (N_FF + 1), 0)
    j = q % STEPS_PER_BATCH
    in_batch = i > N_FF
    gate_outs = (af_ref, bf_ref, abk_ref, bb_ref)

    @pl.when(i < N_FF)
    def _cast():
        _cast_ffn_chunk(i, w1g_ref, w1u_ref, w2_ref, w1g_s, w1u_s, w2_s)

    @pl.when(i < N_WIN)
    def _cast_win():
        win_s[i] = win_ref[...].astype(bf16)

    @pl.when(i == 0)
    def _init():
        seq_s[...] = jnp.zeros(seq_s.shape, f32)
        carry_s[...] = jnp.zeros(carry_s.shape, f32)

    def rows_body(x, act, between=None):
        xn = _rms_norm(x, g1_ref[...]).astype(bf16)
        h1 = x + 0.5 * _swiglu(xn, w1g_s, w1u_s, w2_s, act, between)
        xn2 = _rms_norm(h1, g2_ref[...]).astype(bf16)
        u = [jnp.dot(xn2, win_s[c], preferred_element_type=f32) for c in range(N_WIN)]
        n_lru = LRU_WIDTH // MXU_COLS
        lru = jnp.concatenate(u[0:n_lru], axis=1)
        gate = jnp.concatenate(u[n_lru:2 * n_lru], axis=1)
        ab = [jnp.dot(v.astype(bf16), mix_ref[h], preferred_element_type=f32)
              for h, v in enumerate(u[2 * n_lru:])]
        a_b = jnp.concatenate([p[:, 0:MXU_COLS] for p in ab] + [p[:, MXU_COLS:] for p in ab], axis=1)
        return h1, lru, gate, a_b

    def gate_params():
        nlam = -lam_ref[...]
        sp = jnp.maximum(nlam, 0.0) + jnp.log1p(jnp.exp(-jnp.abs(nlam)))
        return cw_ref[...], cb_ref[...], (-0.25 * RG_LRU_C) * sp[0:1, :], (-0.25 * RG_LRU_C) * sp[1:2, :]

    @pl.when(i == N_FF)
    def _meta():
        _, lru, _, a_b = rows_body(xm_ref[...], actm_ref)
        lrum_s[...] = lru
        abm_ref[...] = a_b

    @pl.when(in_batch & (j < TILES_PER_BATCH))
    def _tile():
        cw, cb, k_f, k_b = gate_params()
        n_slab = LRU_WIDTH // LANES
        piece = ROW_TILE * n_slab // GATE_PIECES

        def between(c):
            if c < GATE_PIECES:
                slab, part = c % n_slab, c // n_slab
                _lru_gate_rows(seq_s, part * piece, piece, cw, cb, wg_ref, bg_ref, k_f, k_b, gate_outs,
                               slabs=(slab,))

        h1, lru, gate, a_b = rows_body(x_ref[...], act_ref, between)
        h1_ref[...] = h1
        gate_ref[...] = _gelu_tanh(gate)
        ab_ref[...] = a_b
        head = jnp.concatenate([jnp.zeros((SUBLANES, LRU_WIDTH), f32), lrum_s[...]], axis=0)
        seq_s[0:SEQ_HEAD, :] = jnp.where(j == 0, head, carry_s[...])
        seq_s[SEQ_HEAD:SEQ_HEAD + ROW_TILE, :] = lru
        carry_s[...] = lru[ROW_TILE - SEQ_HEAD:ROW_TILE, :]

    @pl.when(in_batch & (j == TILES_PER_BATCH))
    def _last_block():
        cw, cb, k_f, k_b = gate_params()
        _lru_gate_rows(seq_s, 0, ROW_TILE, cw, cb, wg_ref, bg_ref, k_f, k_b, gate_outs)
        seq_s[0:SEQ_HEAD, :] = carry_s[...]
        seq_s[SEQ_HEAD:SEQ_HEAD + SUBLANES, :] = jnp.zeros((SUBLANES, LRU_WIDTH), f32)

    @pl.when(in_batch & (j == TILES_PER_BATCH + 1))
    def _tail_block():
        cw, cb, k_f, k_b = gate_params()
        _lru_gate_rows(seq_s, 0, N_META, cw, cb, wg_ref, bg_ref, k_f, k_b, gate_outs)


def _ffn_in(x3d, xm, g1, w_ffn_in, w_ffn_out, g2, w_in, mix, conv_w, conv_b, wg, bg, lam):
    batch, seq, _ = x3d.shape
    assert seq == TILES_PER_BATCH * ROW_TILE
    n_meta = xm.shape[0]
    lead = N_FF + 1

    def tile_index(i):
        q = jnp.maximum(i - lead, 0)
        return q // STEPS_PER_BATCH, jnp.minimum(q % STEPS_PER_BATCH, TILES_PER_BATCH - 1)

    def block_index(i):
        q = jnp.maximum(i - lead, 0)
        return q // STEPS_PER_BATCH, jnp.clip(q % STEPS_PER_BATCH - 1, 0, TILES_PER_BATCH)

    row_spec = lambda w: pl.BlockSpec((None, ROW_TILE, w), lambda i: (*tile_index(i), 0))
    n_slab = LRU_WIDTH // LANES

    def seq_index(i):
        b, blk = block_index(i)
        return b, 0, blk, 0

    seq_spec = pl.BlockSpec((None, n_slab, ROW_TILE, LANES), seq_index)
    whole = lambda shape: pl.BlockSpec(shape, lambda i: (0,) * len(shape))
    coeff = jax.ShapeDtypeStruct((batch, n_slab, n_meta + seq, LANES), f32)
    return pl.pallas_call(
        _ffn_in_kernel,
        grid=(lead + batch * STEPS_PER_BATCH,),
        in_specs=[row_spec(D_MODEL), _resident(xm.shape), _resident((1, D_MODEL)), *_ffn_weight_specs(),
                  _resident((1, D_MODEL)),
                  pl.BlockSpec((D_MODEL, MXU_COLS), lambda i: (0, jnp.minimum(i, N_WIN - 1))),
                  _resident(mix.shape), _resident(conv_w.shape), _resident(conv_b.shape),
                  _resident(wg.shape), _resident(bg.shape), _resident(lam.shape)],
        out_specs=[row_spec(D_MODEL), row_spec(LRU_WIDTH), row_spec(2 * FOURIER_WIDTH),
                   whole((n_meta, 2 * FOURIER_WIDTH)), seq_spec, seq_spec, seq_spec, seq_spec],
        out_shape=[jax.ShapeDtypeStruct((batch, seq, D_MODEL), f32),
                   jax.ShapeDtypeStruct((batch, seq, LRU_WIDTH), f32),
                   jax.ShapeDtypeStruct((batch, seq, 2 * FOURIER_WIDTH), f32),
                   jax.ShapeDtypeStruct((n_meta, 2 * FOURIER_WIDTH), f32),
                   coeff, coeff, coeff, coeff],
        scratch_shapes=_ffn_weight_scratch()
                       + [pltpu.VMEM((N_WIN, D_MODEL, MXU_COLS), bf16),
                          pltpu.VMEM((ROW_TILE, D_FF), bf16), pltpu.VMEM((n_meta, D_FF), bf16),
                          pltpu.VMEM((SEQ_HEAD + ROW_TILE, LRU_WIDTH), f32),
                          pltpu.VMEM((SEQ_HEAD, LRU_WIDTH), f32), pltpu.VMEM((n_meta, LRU_WIDTH), f32)],
        compiler_params=pltpu.CompilerParams(dimension_semantics=("arbitrary",), vmem_limit_bytes=VMEM_LIMIT),
        name="ffn_in",
    )(x3d, xm, g1, w_ffn_in, w_ffn_in, w_ffn_out, g2, w_in, mix, conv_w, conv_b, wg, bg, lam)


def _lru_kernel(af_ref, bf_ref, ab_ref, bb_ref, o_ref, hf_ref, hb_ref, ends_ref, carry_ref):
    t_all = N_SEG * SEG_LEN
    zero8 = jnp.zeros((SUBLANES, LANES), f32)
    half = SUBLANES * SEG_LEN

    def seg_rows(j, hi):
        return pl.ds(j + hi * half, SUBLANES, stride=SEG_LEN)

    def pass1(j, st):
        hf0, hf1, pf0, pf1, hb0, hb1, pb0, pb1 = st
        jb = SEG_LEN - 1 - j
        a0 = af_ref[seg_rows(j, 0), :]
        a1 = af_ref[seg_rows(j, 1), :]
        hf0 = a0 * hf0 + bf_ref[seg_rows(j, 0), :]
        hf1 = a1 * hf1 + bf_ref[seg_rows(j, 1), :]
        pf0 = a0 * pf0
        pf1 = a1 * pf1
        c0 = ab_ref[seg_rows(jb, 0), :]
        c1 = ab_ref[seg_rows(jb, 1), :]
        hb0 = c0 * hb0 + bb_ref[seg_rows(jb, 0), :]
        hb1 = c1 * hb1 + bb_ref[seg_rows(jb, 1), :]
        pb0 = c0 * pb0
        pb1 = c1 * pb1
        return hf0, hf1, pf0, pf1, hb0, hb1, pb0, pb1

    one8 = jnp.ones((SUBLANES, LANES), f32)
    st = lax.fori_loop(0, SEG_LEN, pass1, (zero8, zero8, one8, one8, zero8, zero8, one8, one8),
                       unroll=SCAN_UNROLL)
    for idx, v in enumerate(st):
        ends_ref[idx * SUBLANES:(idx + 1) * SUBLANES, :] = v

    c = jnp.zeros((1, LANES), f32)
    for s in range(N_SEG):
        carry_ref[s:s + 1, :] = c
        c = ends_ref[s:s + 1, :] + ends_ref[N_SEG + s:N_SEG + s + 1, :] * c
    c = jnp.zeros((1, LANES), f32)
    for s in range(N_SEG - 1, -1, -1):
        carry_ref[N_SEG + s:N_SEG + s + 1, :] = c
        c = ends_ref[2 * N_SEG + s:2 * N_SEG + s + 1, :] + ends_ref[3 * N_SEG + s:3 * N_SEG + s + 1, :] * c

    def pass2(j, st):
        hf0, hf1, hb0, hb1 = st
        jb = SEG_LEN - 1 - j
        hf0 = af_ref[seg_rows(j, 0), :] * hf0 + bf_ref[seg_rows(j, 0), :]
        hf1 = af_ref[seg_rows(j, 1), :] * hf1 + bf_ref[seg_rows(j, 1), :]
        hf_ref[seg_rows(j, 0), :] = hf0
        hf_ref[seg_rows(j, 1), :] = hf1
        hb0 = ab_ref[seg_rows(jb, 0), :] * hb0 + bb_ref[seg_rows(jb, 0), :]
        hb1 = ab_ref[seg_rows(jb, 1), :] * hb1 + bb_ref[seg_rows(jb, 1), :]
        hb_ref[seg_rows(jb, 0), :] = hb0
        hb_ref[seg_rows(jb, 1), :] = hb1
        return hf0, hf1, hb0, hb1

    lax.fori_loop(0, SEG_LEN, pass2,
                  (carry_ref[0:SUBLANES, :], carry_ref[SUBLANES:2 * SUBLANES, :],
                   carry_ref[2 * SUBLANES:3 * SUBLANES, :], carry_ref[3 * SUBLANES:4 * SUBLANES, :]),
                  unroll=SCAN_UNROLL)

    o_ref[...] = hf_ref[N_META:t_all, :] + hb_ref[N_META:t_all, :]


def _lru(a_f, b_f, a_b, b_b, seq):
    batch, n_slab, t_all, _ = a_f.shape
    slab = pl.BlockSpec((None, None, t_all, LANES), lambda b, s: (b, s, 0, 0))
    return pl.pallas_call(
        _lru_kernel,
        grid=(batch, n_slab),
        in_specs=[slab] * 4,
        out_specs=pl.BlockSpec((None, seq, LANES), lambda b, s: (b, 0, s)),
        out_shape=jax.ShapeDtypeStruct((batch, seq, LRU_WIDTH), f32),
        scratch_shapes=[pltpu.VMEM((t_all, LANES), f32)] * 2
                       + [pltpu.VMEM((4 * N_SEG, LANES), f32), pltpu.VMEM((2 * N_SEG, LANES), f32)],
        compiler_params=pltpu.CompilerParams(dimension_semantics=("arbitrary", "arbitrary"),
                                             vmem_limit_bytes=VMEM_LIMIT),
        name="lru",
    )(a_f, b_f, a_b, b_b)


def _cmul_const(x, c, s):
    re, im = x
    tol = 1e-12
    if abs(s) < tol:
        return (re, im) if c > 0 else (-re, -im)
    if abs(c) < tol:
        return (-im, re) if s > 0 else (im, -re)
    if abs(abs(c) - abs(s)) < tol:
        m = abs(c)
        sc, ss = (1.0 if c > 0 else -1.0), (1.0 if s > 0 else -1.0)
        return (m * (sc * re - ss * im), m * (ss * re + sc * im))
    return (c * re - s * im, s * re + c * im)


def _fft(xs):
    n = len(xs)
    if n == 1:
        return xs
    ev = _fft(xs[0::2])
    od = _fft(xs[1::2])
    out = [None] * n
    for k in range(n // 2):
        ang = -2.0 * math.pi * k / n
        t = _cmul_const(od[k], math.cos(ang), math.sin(ang))
        out[k] = (ev[k][0] + t[0], ev[k][1] + t[1])
        out[k + n // 2] = (ev[k][0] - t[0], ev[k][1] - t[1])
    return out


def _fourier_kernel(xa_ref, xb_ref, ma_ref, mb_ref, fm_ref, twc_ref, tws_ref, o_ref,
                    ha_ref, hb_ref, ta_ref, tb_ref, y_ref):
    tail0 = (N_SLAB - 1) * SLAB - N_META
    for m_ref, x_ref, h_ref, t_ref in ((ma_ref, xa_ref, ha_ref, ta_ref), (mb_ref, xb_ref, hb_ref, tb_ref)):
        h_ref[0:N_META, :] = m_ref[...]
        h_ref[N_META:SLAB_PAD, :] = x_ref[0:SLAB_PAD - N_META, :]
        t_ref[SLAB - 1:SLAB_PAD, :] = jnp.zeros((SLAB_PAD - SLAB + 1, LANES), f32)
        t_ref[0:SLAB - 1, :] = x_ref[tail0:tail0 + SLAB - 1, :]
        t_ref[SLAB - 1:SLAB, :] = x_ref[tail0 + SLAB - 1:tail0 + SLAB, :]

    def stage1(it, carry):
        r = pl.multiple_of(it * SUBLANES, SUBLANES)
        rows = pl.ds(r, SUBLANES)

        def slab_rows(x_ref, h_ref, t_ref, t1):
            if t1 == 0:
                return h_ref[rows, :]
            if t1 == N_SLAB - 1:
                return t_ref[rows, :]
            return x_ref[pl.ds(r + (t1 * SLAB - N_META), SUBLANES), :]

        zs = [(slab_rows(xa_ref, ha_ref, ta_ref, t1), slab_rows(xb_ref, hb_ref, tb_ref, t1))
              for t1 in range(N_SLAB)]
        ys = _fft(zs)
        for k1 in range(N_SLAB):
            yr, yi = ys[k1]
            if k1 > 0:
                cw = twc_ref[k1, rows, :]
                sw = tws_ref[k1, rows, :]
                yr, yi = yr * cw + yi * sw, yi * cw - yr * sw
            y_ref[rows, k1 * LANES:(k1 + 1) * LANES] = yr
            y_ref[pl.ds(r + SLAB_PAD, SUBLANES), k1 * LANES:(k1 + 1) * LANES] = yi
        return carry

    lax.fori_loop(0, SLAB_PAD // SUBLANES, stage1, 0)

    for p in range(N_SLAB // 2):
        yp = y_ref[:, 2 * p * LANES:(2 * p + 2) * LANES].astype(bf16)
        g = jnp.dot(fm_ref[...], yp, preferred_element_type=f32)
        o_ref[pl.ds(2 * p, SLAB - 1, stride=N_SLAB), :] = g[:, 0:LANES]
        o_ref[pl.ds(2 * p + 1, SLAB - 1, stride=N_SLAB), :] = g[:, LANES:2 * LANES]


def _fourier(ab, ab_m, fm, twc, tws, batch, seq):
    n_blk = FOURIER_WIDTH // LANES
    x_spec = lambda off: pl.BlockSpec((None, seq, LANES), lambda b, j: (b, 0, j + off))
    m_spec = lambda off: pl.BlockSpec((N_META, LANES), lambda b, j: (0, j + off))
    return pl.pallas_call(
        _fourier_kernel,
        grid=(batch, n_blk),
        in_specs=[x_spec(0), x_spec(n_blk), m_spec(0), m_spec(n_blk),
                  _resident(fm.shape), _resident(twc.shape), _resident(tws.shape)],
        out_specs=pl.BlockSpec((None, seq, LANES), lambda b, j: (b, 0, j)),
        out_shape=jax.ShapeDtypeStruct((batch, seq, FOURIER_WIDTH), f32),
        scratch_shapes=[pltpu.VMEM((SLAB_PAD, LANES), f32)] * 4
                       + [pltpu.VMEM((2 * SLAB_PAD, N_SLAB * LANES), f32)],
        compiler_params=pltpu.CompilerParams(dimension_semantics=("arbitrary", "arbitrary"),
                                             vmem_limit_bytes=VMEM_LIMIT),
        name="fourier",
    )(ab, ab, ab_m, ab_m, fm, twc, tws)


def _fourier_constants():
    n = N_SLAB * SLAB
    k2 = np.arange(1, SLAB)[:, None]
    t2 = np.arange(SLAB_PAD)[None, :]
    ang = 2.0 * np.pi * ((k2 * t2) % SLAB) / SLAB
    valid = (t2 < SLAB)
    fm = np.concatenate([np.cos(ang) * valid, np.sin(ang) * valid], axis=1)
    k1 = np.arange(N_SLAB)[:, None]
    ang2 = 2.0 * np.pi * ((k1 * t2) % n) / n
    lane_bcast = np.ones((1, 1, LANES))
    twc = (np.cos(ang2) * valid)[:, :, None] * lane_bcast
    tws = (np.sin(ang2) * valid)[:, :, None] * lane_bcast
    return (jnp.asarray(fm, f32), jnp.asarray(twc, f32), jnp.asarray(tws, f32))


def _mix_out_kernel(h1_ref, hs_ref, gate_ref, four_ref, gl_ref, fb_ref, gf_ref, wo_ref,
                    g2_ref, w1g_ref, w1u_ref, w2_ref, gfin_ref, o_ref,
                    wo_s, w1g_s, w1u_s, w2_s, mix_ref, act_ref):
    i = pl.program_id(0)

    @pl.when(i < N_FF)
    def _cast():
        _cast_ffn_chunk(i, w1g_ref, w1u_ref, w2_ref, w1g_s, w1u_s, w2_s)

    @pl.when(i < N_WO)
    def _cast_wo():
        wo_s[i] = wo_ref[...].astype(bf16)

    @pl.when(i >= N_FF)
    def _tile():
        y_l = _rms_norm(hs_ref[...] * gate_ref[...], gl_ref[...])
        y_f = _rms_norm(four_ref[...] + fb_ref[...], gf_ref[...])
        mix_ref[:, 0:LRU_WIDTH] = y_l.astype(bf16)
        mix_ref[:, LRU_WIDTH:] = y_f.astype(bf16)
        mixed = mix_ref[...]
        proj = jnp.concatenate([jnp.dot(mixed, wo_s[c], preferred_element_type=f32) for c in range(N_WO)], axis=1)
        h2 = h1_ref[...] + proj
        xn = _rms_norm(h2, g2_ref[...]).astype(bf16)
        h3 = h2 + 0.5 * _swiglu(xn, w1g_s, w1u_s, w2_s, act_ref)
        o_ref[...] = _rms_norm(h3, gfin_ref[...])


def _mix_out(h1, hs, gate, four, gl, fb, gf, w_out, g2, w_ffn_in, w_ffn_out, gfin, tile):
    rows = h1.shape[0]
    row_spec = lambda w: pl.BlockSpec((tile, w), lambda i: (jnp.maximum(i - N_FF, 0), 0))
    return pl.pallas_call(
        _mix_out_kernel,
        grid=(N_FF + rows // tile,),
        in_specs=[row_spec(D_MODEL), row_spec(LRU_WIDTH), row_spec(LRU_WIDTH), row_spec(FOURIER_WIDTH),
                  _resident((1, LRU_WIDTH)), _resident((1, FOURIER_WIDTH)), _resident((1, FOURIER_WIDTH)),
                  pl.BlockSpec((D_MODEL, MXU_COLS), lambda i: (0, jnp.minimum(i, N_WO - 1))),
                  _resident((1, D_MODEL)), *_ffn_weight_specs(), _resident((1, D_MODEL))],
        out_specs=row_spec(D_MODEL),
        out_shape=jax.ShapeDtypeStruct((rows, D_MODEL), f32),
        scratch_shapes=[pltpu.VMEM((N_WO, D_MODEL, MXU_COLS), bf16)] + _ffn_weight_scratch()
                       + [pltpu.VMEM((tile, D_MODEL), bf16), pltpu.VMEM((tile, D_FF), bf16)],
        compiler_params=pltpu.CompilerParams(dimension_semantics=("arbitrary",), vmem_limit_bytes=VMEM_LIMIT),
        name="mix_out",
    )(h1, hs, gate, four, gl, fb, gf, w_out, g2, w_ffn_in, w_ffn_in, w_ffn_out, gfin)


def _gate_weights(wa_f, wx_f, wa_b, wx_b, ba_f, bx_f, ba_b, bx_b):
    n_slab = LRU_WIDTH // LANES
    per = LANES // LRU_HEAD_DIM
    eye = jnp.eye(per, dtype=f32)

    def slab_bd(w):
        w = w.astype(f32).reshape(n_slab, per, LRU_HEAD_DIM, LRU_HEAD_DIM)
        return (eye[None, :, None, :, None] * w[:, :, :, None, :]).reshape(n_slab, LANES, LANES)

    wg = jnp.concatenate([slab_bd(w) for w in (wa_f, wx_f, wa_b, wx_b)], axis=-1)
    bg = jnp.concatenate([b.astype(f32).reshape(n_slab, 1, LANES) for b in (ba_f, bx_f, ba_b, bx_b)], axis=-1)
    return (0.5 * wg).astype(bf16), 0.5 * bg


def kernel(x, meta_tokens, norm_ffn1, w_ffn1_in, w_ffn1_out, norm_mix, w_in, conv_w, conv_b, lru_wa_fwd, lru_ba_fwd, lru_wx_fwd, lru_bx_fwd, lru_lambda_fwd, lru_wa_bwd, lru_ba_bwd, lru_wx_bwd, lru_bx_bwd, lru_lambda_bwd, fourier_w, fourier_b, norm_lru_out, norm_fourier_out, w_out, norm_ffn2, w_ffn2_in, w_ffn2_out, norm_final):
    batch, seq, d = x.shape
    assert (d, seq + N_META) == (D_MODEL, N_SLAB * SLAB) and norm_ffn1.shape[0] == 1
    l = 0
    row = lambda v: v.astype(f32).reshape(1, -1)

    mix = _fourier_channel_mix(fourier_w[l]).astype(bf16)
    wg, bg = _gate_weights(lru_wa_fwd[l], lru_wx_fwd[l], lru_wa_bwd[l], lru_wx_bwd[l],
                           lru_ba_fwd[l], lru_bx_fwd[l], lru_ba_bwd[l], lru_bx_bwd[l])
    lam = jnp.stack([lru_lambda_fwd[l], lru_lambda_bwd[l]]).astype(f32)

    h1, gate, ab, ab_m, a_f, b_f, a_b, b_b = _ffn_in(
        x.astype(f32), meta_tokens.astype(f32), row(norm_ffn1[l]),
        w_ffn1_in[l].astype(f32), w_ffn1_out[l].astype(f32), row(norm_mix[l]), w_in[l].astype(f32), mix,
        conv_w[l].astype(f32), row(conv_b[l]), wg, bg, lam)

    hsum = _lru(a_f, b_f, a_b, b_b, seq)

    fm, twc, tws = _fourier_constants()
    four = _fourier(ab, ab_m, fm.astype(bf16), twc, tws, batch, seq)

    out = _mix_out(h1.reshape(batch * seq, d), hsum.reshape(batch * seq, LRU_WIDTH),
                   gate.reshape(batch * seq, LRU_WIDTH), four.reshape(batch * seq, FOURIER_WIDTH),
                   row(norm_lru_out[l]), row(fourier_b[l]), row(norm_fourier_out[l]), w_out[l].astype(f32),
                   row(norm_ffn2[l]), w_ffn2_in[l].astype(f32), w_ffn2_out[l].astype(f32), row(norm_final),
                   tile=ROW_TILE)
    return out.reshape(batch, seq, d).astype(x.dtype)
```

```python
import math

import numpy as np
import jax
import jax.numpy as jnp
from jax import lax
from jax.experimental import pallas as pl
from jax.experimental.pallas import tpu as pltpu

D_MODEL = 1024
N_META = 16
LRU_WIDTH = 512
LRU_HEAD_DIM = 64
FOURIER_WIDTH = 512
FOURIER_GROUPS = 8
FOURIER_GROUP_DIM = 64
D_FF = 2816
EPS = 1e-6
RG_LRU_C = 8.0

LANES = 128
SUBLANES = 8
ROW_TILE = 512
TILES_PER_BATCH = 16
STEPS_PER_BATCH = TILES_PER_BATCH + 2
GATE_PIECES = 8
SEQ_HEAD = 24
MXU_COLS = 256
N_FF = D_FF // MXU_COLS
N_WIN = (2 * LRU_WIDTH + FOURIER_WIDTH) // MXU_COLS
N_WO = D_MODEL // MXU_COLS
VMEM_LIMIT = 60 * 1024 * 1024

N_SLAB = N_META
SLAB = 513
SLAB_PAD = 520
N_SEG = 16
SEG_LEN = 513
SCAN_UNROLL = 9

f32 = jnp.float32
bf16 = jnp.bfloat16


def _rms_norm(x, g):
    return x * lax.rsqrt(jnp.mean(x * x, axis=-1, keepdims=True) + EPS) * g


def _sigmoid(x):
    return 0.5 * (1.0 + jnp.tanh(0.5 * x))


def _gelu_tanh(x):
    c = math.sqrt(2.0 / math.pi)
    return 0.5 * x * (1.0 + jnp.tanh(c * (x + 0.044715 * (x * x * x))))


def _swiglu(xn, w1g_s, w1u_s, w2_s, act_ref, between=None):
    for c in range(N_FF):
        g = jnp.dot(xn, w1g_s[c], preferred_element_type=f32)
        u = jnp.dot(xn, w1u_s[c], preferred_element_type=f32)
        act_ref[:, c * MXU_COLS:(c + 1) * MXU_COLS] = (g * _sigmoid(g) * u).astype(bf16)
        if between is not None:
            between(c)
    return jnp.dot(act_ref[...], w2_s[...], preferred_element_type=f32)


def _cast_ffn_chunk(i, w1g_ref, w1u_ref, w2_ref, w1g_s, w1u_s, w2_s):
    w1g_s[i] = w1g_ref[...].astype(bf16)
    w1u_s[i] = w1u_ref[...].astype(bf16)
    w2_s[pl.ds(pl.multiple_of(i * MXU_COLS, MXU_COLS), MXU_COLS), :] = w2_ref[...].astype(bf16)


def _ffn_weight_specs():
    last = N_FF - 1
    return [pl.BlockSpec((D_MODEL, MXU_COLS), lambda i: (0, jnp.minimum(i, last))),
            pl.BlockSpec((D_MODEL, MXU_COLS), lambda i: (0, N_FF + jnp.minimum(i, last))),
            pl.BlockSpec((MXU_COLS, D_MODEL), lambda i: (jnp.minimum(i, last), 0))]


def _ffn_weight_scratch():
    return [pltpu.VMEM((N_FF, D_MODEL, MXU_COLS), bf16), pltpu.VMEM((N_FF, D_MODEL, MXU_COLS), bf16),
            pltpu.VMEM((D_FF, D_MODEL), bf16)]


def _resident(shape):
    nd = len(shape)
    return pl.BlockSpec(shape, lambda *_: (0,) * nd, pipeline_mode=pl.Buffered(1))


def _chanmix_kernel(bd_ref, cd_ref, sd_ref, o_ref):
    hi = lax.Precision.HIGHEST
    for h in range(FOURIER_WIDTH // MXU_COLS):
        blk = slice(h * MXU_COLS, (h + 1) * MXU_COLS)
        bd = bd_ref[blk, blk]
        o_ref[h, :, 0:MXU_COLS] = jnp.dot(cd_ref[blk, blk], bd, preferred_element_type=f32,
                                          precision=hi).astype(o_ref.dtype)
        o_ref[h, :, MXU_COLS:] = jnp.dot(sd_ref[blk, blk], bd, preferred_element_type=f32,
                                         precision=hi).astype(o_ref.dtype)


def _fourier_channel_mix(fourier_w):
    g, n = FOURIER_GROUPS, FOURIER_GROUP_DIM
    eye = np.eye(g, dtype=np.float64)
    ang = 2.0 * np.pi * (np.outer(np.arange(n), np.arange(n)) % n) / n
    scale = 1.0 / math.sqrt((N_SLAB * SLAB) * n)
    cd = jnp.asarray(np.kron(eye, np.cos(ang)) * scale, f32)
    sd = jnp.asarray(np.kron(eye, -np.sin(ang)) * scale, f32)
    bd = jnp.tile(fourier_w.astype(f32).reshape(g * n, n), (1, g)) * jnp.asarray(np.kron(eye, np.ones((n, n))), f32)
    return pl.pallas_call(
        _chanmix_kernel,
        out_shape=jax.ShapeDtypeStruct((FOURIER_WIDTH // MXU_COLS, MXU_COLS, 2 * MXU_COLS), bf16),
        name="fourier_channel_mix",
    )(bd, cd, sd)


def _lru_gate_rows(seq_s, r0, n, cw, cb, wg_ref, bg_ref, k_f, k_b, outs, slabs=range(LRU_WIDTH // LANES)):
    af_ref, bf_ref, ab_ref, bb_ref = outs

    def gates(half_pre_r, half_pre_i, x, k_dir):
        t = jnp.tanh(k_dir * jnp.tanh(half_pre_r) + k_dir)
        q = 1.0 / (1.0 - t)
        a = (1.0 + t) * q
        y = -t
        root = jnp.where(y > 0.0, y * lax.rsqrt(y), 0.0)
        return a, (root * q) * ((1.0 + jnp.tanh(half_pre_i)) * x)

    for s in slabs:
        lanes = slice(s * LANES, (s + 1) * LANES)
        x = (cw[0:1, lanes] * seq_s[r0 + 6:r0 + 6 + n, lanes] + cw[1:2, lanes] * seq_s[r0 + 7:r0 + 7 + n, lanes]
             + cw[2:3, lanes] * seq_s[r0 + 8:r0 + 8 + n, lanes] + cw[3:4, lanes] * seq_s[r0 + 9:r0 + 9 + n, lanes]
             + cb[:, lanes])
        pre = jnp.dot(x.astype(bf16), wg_ref[s], preferred_element_type=f32) + bg_ref[s]
        a_f, b_f = gates(pre[:, 0:LANES], pre[:, LANES:2 * LANES], x, k_f[:, lanes])
        a_b, b_b = gates(pre[:, 2 * LANES:3 * LANES], pre[:, 3 * LANES:4 * LANES], x, k_b[:, lanes])
        af_ref[s, r0:r0 + n, :] = a_f
        bf_ref[s, r0:r0 + n, :] = b_f
        ab_ref[s, r0:r0 + n, :] = a_b
        bb_ref[s, r0:r0 + n, :] = b_b


def _ffn_in_kernel(x_ref, xm_ref, g1_ref, w1g_ref, w1u_ref, w2_ref, g2_ref, win_ref, mix_ref,
                   cw_ref, cb_ref, wg_ref, bg_ref, lam_ref,
                   h1_ref, gate_ref, ab_ref, abm_ref, af_ref, bf_ref, abk_ref, bb_ref,
                   w1g_s, w1u_s, w2_s, win_s, act_ref, actm_ref, seq_s, carry_s, lrum_s):
    i = pl.program_id(0)
    q = jnp.maximum(i - (N_FF + 1), 0)
    j = q % STEPS_PER_BATCH
    in_batch = i > N_FF
    gate_outs = (af_ref, bf_ref, abk_ref, bb_ref)

    @pl.when(i < N_FF)
    def _cast():
        _cast_ffn_chunk(i, w1g_ref, w1u_ref, w2_ref, w1g_s, w1u_s, w2_s)

    @pl.when(i < N_WIN)
    def _cast_win():
        win_s[i] = win_ref[...].astype(bf16)

    @pl.when(i == 0)
    def _init():
        seq_s[...] = jnp.zeros(seq_s.shape, f32)
        carry_s[...] = jnp.zeros(carry_s.shape, f32)

    def rows_body(x, act, between=None):
        xn = _rms_norm(x, g1_ref[...]).astype(bf16)
        h1 = x + 0.5 * _swiglu(xn, w1g_s, w1u_s, w2_s, act, between)
        xn2 = _rms_norm(h1, g2_ref[...]).astype(bf16)
        u = [jnp.dot(xn2, win_s[c], preferred_element_type=f32) for c in range(N_WIN)]
        n_lru = LRU_WIDTH // MXU_COLS
        lru = jnp.concatenate(u[0:n_lru], axis=1)
        gate = jnp.concatenate(u[n_lru:2 * n_lru], axis=1)
        ab = [jnp.dot(v.astype(bf16), mix_ref[h], preferred_element_type=f32)
              for h, v in enumerate(u[2 * n_lru:])]
        a_b = jnp.concatenate([p[:, 0:MXU_COLS] for p in ab] + [p[:, MXU_COLS:] for p in ab], axis=1)
        return h1, lru, gate, a_b

    def gate_params():
        nlam = -lam_ref[...]
        sp = jnp.maximum(nlam, 0.0) + jnp.log1p(jnp.exp(-jnp.abs(nlam)))
        return cw_ref[...], cb_ref[...], (-0.25 * RG_LRU_C) * sp[0:1, :], (-0.25 * RG_LRU_C) * sp[1:2, :]

    @pl.when(i == N_FF)
    def _meta():
        _, lru, _, a_b = rows_body(xm_ref[...], actm_ref)
        lrum_s[...] = lru
        abm_ref[...] = a_b

    @pl.when(in_batch & (j < TILES_PER_BATCH))
    def _tile():
        cw, cb, k_f, k_b = gate_params()
        n_slab = LRU_WIDTH // LANES
        piece = ROW_TILE * n_slab // GATE_PIECES

        def between(c):
            if c < GATE_PIECES:
                slab, part = c % n_slab, c // n_slab
                _lru_gate_rows(seq_s, part * piece, piece, cw, cb, wg_ref, bg_ref, k_f, k_b, gate_outs,
                               slabs=(slab,))

        h1, lru, gate, a_b = rows_body(x_ref[...], act_ref, between)
        h1_ref[...] = h1
        gate_ref[...] = _gelu_tanh(gate)
        ab_ref[...] = a_b
        head = jnp.concatenate([jnp.zeros((SUBLANES, LRU_WIDTH), f32), lrum_s[...]], axis=0)
        seq_s[0:SEQ_HEAD, :] = jnp.where(j == 0, head, carry_s[...])
        seq_s[SEQ_HEAD:SEQ_HEAD + ROW_TILE, :] = lru
        carry_s[...] = lru[ROW_TILE - SEQ_HEAD:ROW_TILE, :]

    @pl.when(in_batch & (j == TILES_PER_BATCH))
    def _last_block():
        cw, cb, k_f, k_b = gate_params()
        _lru_gate_rows(seq_s, 0, ROW_TILE, cw, cb, wg_ref, bg_ref, k_f, k_b, gate_outs)
        seq_s[0:SEQ_HEAD, :] = carry_s[...]
        seq_s[SEQ_HEAD:SEQ_HEAD + SUBLANES, :] = jnp.zeros((SUBLANES, LRU_WIDTH), f32)

    @pl.when(in_batch & (j == TILES_PER_BATCH + 1))
    def _tail_block():
        cw, cb, k_f, k_b = gate_params()
        _lru_gate_rows(seq_s, 0, N_META, cw, cb, wg_ref, bg_ref, k_f, k_b, gate_outs)


def _ffn_in(x3d, xm, g1, w_ffn_in, w_ffn_out, g2, w_in, mix, conv_w, conv_b, wg, bg, lam):
    batch, seq, _ = x3d.shape
    assert seq == TILES_PER_BATCH * ROW_TILE
    n_meta = xm.shape[0]
    lead = N_FF + 1

    def tile_index(i):
        q = jnp.maximum(i - lead, 0)
        return q // STEPS_PER_BATCH, jnp.minimum(q % STEPS_PER_BATCH, TILES_PER_BATCH - 1)

    def block_index(i):
        q = jnp.maximum(i - lead, 0)
        return q // STEPS_PER_BATCH, jnp.clip(q % STEPS_PER_BATCH - 1, 0, TILES_PER_BATCH)

    row_spec = lambda w: pl.BlockSpec((None, ROW_TILE, w), lambda i: (*tile_index(i), 0))
    n_slab = LRU_WIDTH // LANES

    def seq_index(i):
        b, blk = block_index(i)
        return b, 0, blk, 0

    seq_spec = pl.BlockSpec((None, n_slab, ROW_TILE, LANES), seq_index)
    whole = lambda shape: pl.BlockSpec(shape, lambda i: (0,) * len(shape))
    coeff = jax.ShapeDtypeStruct((batch, n_slab, n_meta + seq, LANES), f32)
    return pl.pallas_call(
        _ffn_in_kernel,
        grid=(lead + batch * STEPS_PER_BATCH,),
        in_specs=[row_spec(D_MODEL), _resident(xm.shape), _resident((1, D_MODEL)), *_ffn_weight_specs(),
                  _resident((1, D_MODEL)),
                  pl.BlockSpec((D_MODEL, MXU_COLS), lambda i: (0, jnp.minimum(i, N_WIN - 1))),
                  _resident(mix.shape), _resident(conv_w.shape), _resident(conv_b.shape),
                  _resident(wg.shape), _resident(bg.shape), _resident(lam.shape)],
        out_specs=[row_spec(D_MODEL), row_spec(LRU_WIDTH), row_spec(2 * FOURIER_WIDTH),
                   whole((n_meta, 2 * FOURIER_WIDTH)), seq_spec, seq_spec, seq_spec, seq_spec],
        out_shape=[jax.ShapeDtypeStruct((batch, seq, D_MODEL), f32),
                   jax.ShapeDtypeStruct((batch, seq, LRU_WIDTH), f32),
                   jax.ShapeDtypeStruct((batch, seq, 2 * FOURIER_WIDTH), f32),
                   jax.ShapeDtypeStruct((n_meta, 2 * FOURIER_WIDTH), f32),
                   coeff, coeff, coeff, coeff],
        scratch_shapes=_ffn_weight_scratch()
                       + [pltpu.VMEM((N_WIN, D_MODEL, MXU_COLS), bf16),
                          pltpu.VMEM((ROW_TILE, D_FF), bf16), pltpu.VMEM((n_meta, D_FF), bf16),
                          pltpu.VMEM((SEQ_HEAD + ROW_TILE, LRU_WIDTH), f32),
                          pltpu.VMEM((SEQ_HEAD, LRU_WIDTH), f32), pltpu.VMEM((n_meta, LRU_WIDTH), f32)],
        compiler_params=pltpu.CompilerParams(dimension_semantics=("arbitrary",), vmem_limit_bytes=VMEM_LIMIT),
        name="ffn_in",
    )(x3d, xm, g1, w_ffn_in, w_ffn_in, w_ffn_out, g2, w_in, mix, conv_w, conv_b, wg, bg, lam)


def _lru_kernel(af_ref, bf_ref, ab_ref, bb_ref, o_ref, hf_ref, hb_ref, ends_ref, carry_ref):
    t_all = N_SEG * SEG_LEN
    zero8 = jnp.zeros((SUBLANES, LANES), f32)
    half = SUBLANES * SEG_LEN

    def seg_rows(j, hi):
        return pl.ds(j + hi * half, SUBLANES, stride=SEG_LEN)

    def pass1(j, st):
        hf0, hf1, pf0, pf1, hb0, hb1, pb0, pb1 = st
        jb = SEG_LEN - 1 - j
        a0 = af_ref[seg_rows(j, 0), :]
        a1 = af_ref[seg_rows(j, 1), :]
        hf0 = a0 * hf0 + bf_ref[seg_rows(j, 0), :]
        hf1 = a1 * hf1 + bf_ref[seg_rows(j, 1), :]
        pf0 = a0 * pf0
        pf1 = a1 * pf1
        c0 = ab_ref[seg_rows(jb, 0), :]
        c1 = ab_ref[seg_rows(jb, 1), :]
        hb0 = c0 * hb0 + bb_ref[seg_rows(jb, 0), :]
        hb1 = c1 * hb1 + bb_ref[seg_rows(jb, 1), :]
        pb0 = c0 * pb0
        pb1 = c1 * pb1
        return hf0, hf1, pf0, pf1, hb0, hb1, pb0, pb1

    one8 = jnp.ones((SUBLANES, LANES), f32)
    st = lax.fori_loop(0, SEG_LEN, pass1, (zero8, zero8, one8, one8, zero8, zero8, one8, one8),
                       unroll=SCAN_UNROLL)
    for idx, v in enumerate(st):
        ends_ref[idx * SUBLANES:(idx + 1) * SUBLANES, :] = v

    c = jnp.zeros((1, LANES), f32)
    for s in range(N_SEG):
        carry_ref[s:s + 1, :] = c
        c = ends_ref[s:s + 1, :] + ends_ref[N_SEG + s:N_SEG + s + 1, :] * c
    c = jnp.zeros((1, LANES), f32)
    for s in range(N_SEG - 1, -1, -1):
        carry_ref[N_SEG + s:N_SEG + s + 1, :] = c
        c = ends_ref[2 * N_SEG + s:2 * N_SEG + s + 1, :] + ends_ref[3 * N_SEG + s:3 * N_SEG + s + 1, :] * c

    def pass2(j, st):
        hf0, hf1, hb0, hb1 = st
        jb = SEG_LEN - 1 - j
        hf0 = af_ref[seg_rows(j, 0), :] * hf0 + bf_ref[seg_rows(j, 0), :]
        hf1 = af_ref[seg_rows(j, 1), :] * hf1 + bf_ref[seg_rows(j, 1), :]
        hf_ref[seg_rows(j, 0), :] = hf0
        hf_ref[seg_rows(j, 1), :] = hf1
        hb0 = ab_ref[seg_rows(jb, 0), :] * hb0 + bb_ref[seg_rows(jb, 0), :]
        hb1 = ab_ref[seg_rows(jb, 1), :] * hb1 + bb_ref[seg_rows(jb, 1), :]
        hb_ref[seg_rows(jb, 0), :] = hb0
        hb_ref[seg_rows(jb, 1), :] = hb1
        return hf0, hf1, hb0, hb1

    lax.fori_loop(0, SEG_LEN, pass2,
                  (carry_ref[0:SUBLANES, :], carry_ref[SUBLANES:2 * SUBLANES, :],
                   carry_ref[2 * SUBLANES:3 * SUBLANES, :], carry_ref[3 * SUBLANES:4 * SUBLANES, :]),
                  unroll=SCAN_UNROLL)

    o_ref[...] = hf_ref[N_META:t_all, :] + hb_ref[N_META:t_all, :]


def _lru(a_f, b_f, a_b, b_b, seq):
    batch, n_slab, t_all, _ = a_f.shape
    slab = pl.BlockSpec((None, None, t_all, LANES), lambda b, s: (b, s, 0, 0))
    return pl.pallas_call(
        _lru_kernel,
        grid=(batch, n_slab),
        in_specs=[slab] * 4,
        out_specs=pl.BlockSpec((None, seq, LANES), lambda b, s: (b, 0, s)),
        out_shape=jax.ShapeDtypeStruct((batch, seq, LRU_WIDTH), f32),
        scratch_shapes=[pltpu.VMEM((t_all, LANES), f32)] * 2
                       + [pltpu.VMEM((4 * N_SEG, LANES), f32), pltpu.VMEM((2 * N_SEG, LANES), f32)],
        compiler_params=pltpu.CompilerParams(dimension_semantics=("arbitrary", "arbitrary"),
                                             vmem_limit_bytes=VMEM_LIMIT),
        name="lru",
    )(a_f, b_f, a_b, b_b)


def _cmul_const(x, c, s):
    re, im = x
    tol = 1e-12
    if abs(s) < tol:
        return (re, im) if c > 0 else (-re, -im)
    if abs(c) < tol:
        return (-im, re) if s > 0 else (im, -re)
    if abs(abs(c) - abs(s)) < tol:
        m = abs(c)
        sc, ss = (1.0 if c > 0 else -1.0), (1.0 if s > 0 else -1.0)
        return (m * (sc * re - ss * im), m * (ss * re + sc * im))
    return (c * re - s * im, s * re + c * im)


def _fft(xs):
    n = len(xs)
    if n == 1:
        return xs
    ev = _fft(xs[0::2])
    od = _fft(xs[1::2])
    out = [None] * n
    for k in range(n // 2):
        ang = -2.0 * math.pi * k / n
        t = _cmul_const(od[k], math.cos(ang), math.sin(ang))
        out[k] = (ev[k][0] + t[0], ev[k][1] + t[1])
        out[k + n // 2] = (ev[k][0] - t[0], ev[k][1] - t[1])
    return out


def _fourier_kernel(xa_ref, xb_ref, ma_ref, mb_ref, fm_ref, twc_ref, tws_ref, o_ref,
                    ha_ref, hb_ref, ta_ref, tb_ref, y_ref):
    tail0 = (N_SLAB - 1) * SLAB - N_META
    for m_ref, x_ref, h_ref, t_ref in ((ma_ref, xa_ref, ha_ref, ta_ref), (mb_ref, xb_ref, hb_ref, tb_ref)):
        h_ref[0:N_META, :] = m_ref[...]
        h_ref[N_META:SLAB_PAD, :] = x_ref[0:SLAB_PAD - N_META, :]
        t_ref[SLAB - 1:SLAB_PAD, :] = jnp.zeros((SLAB_PAD - SLAB + 1, LANES), f32)
        t_ref[0:SLAB - 1, :] = x_ref[tail0:tail0 + SLAB - 1, :]
        t_ref[SLAB - 1:SLAB, :] = x_ref[tail0 + SLAB - 1:tail0 + SLAB, :]

    def stage1(it, carry):
        r = pl.multiple_of(it * SUBLANES, SUBLANES)
        rows = pl.ds(r, SUBLANES)

        def slab_rows(x_ref, h_ref, t_ref, t1):
            if t1 == 0:
                return h_ref[rows, :]
            if t1 == N_SLAB - 1:
                return t_ref[rows, :]
            return x_ref[pl.ds(r + (t1 * SLAB - N_META), SUBLANES), :]

        zs = [(slab_rows(xa_ref, ha_ref, ta_ref, t1), slab_rows(xb_ref, hb_ref, tb_ref, t1))
              for t1 in range(N_SLAB)]
        ys = _fft(zs)
        for k1 in range(N_SLAB):
            yr, yi = ys[k1]
            if k1 > 0:
                cw = twc_ref[k1, rows, :]
                sw = tws_ref[k1, rows, :]
                yr, yi = yr * cw + yi * sw, yi * cw - yr * sw
            y_ref[rows, k1 * LANES:(k1 + 1) * LANES] = yr
            y_ref[pl.ds(r + SLAB_PAD, SUBLANES), k1 * LANES:(k1 + 1) * LANES] = yi
        return carry

    lax.fori_loop(0, SLAB_PAD // SUBLANES, stage1, 0)

    for p in range(N_SLAB // 2):
        yp = y_ref[:, 2 * p * LANES:(2 * p + 2) * LANES].astype(bf16)
        g = jnp.dot(fm_ref[...], yp, preferred_element_type=f32)
        o_ref[pl.ds(2 * p, SLAB - 1, stride=N_SLAB), :] = g[:, 0:LANES]
        o_ref[pl.ds(2 * p + 1, SLAB - 1, stride=N_SLAB), :] = g[:, LANES:2 * LANES]


def _fourier(ab, ab_m, fm, twc, tws, batch, seq):
    n_blk = FOURIER_WIDTH // LANES
    x_spec = lambda off: pl.BlockSpec((None, seq, LANES), lambda b, j: (b, 0, j + off))
    m_spec = lambda off: pl.BlockSpec((N_META, LANES), lambda b, j: (0, j + off))
    return pl.pallas_call(
        _fourier_kernel,
        grid=(batch, n_blk),
        in_specs=[x_spec(0), x_spec(n_blk), m_spec(0), m_spec(n_blk),
                  _resident(fm.shape), _resident(twc.shape), _resident(tws.shape)],
        out_specs=pl.BlockSpec((None, seq, LANES), lambda b, j: (b, 0, j)),
        out_shape=jax.ShapeDtypeStruct((batch, seq, FOURIER_WIDTH), f32),
        scratch_shapes=[pltpu.VMEM((SLAB_PAD, LANES), f32)] * 4
                       + [pltpu.VMEM((2 * SLAB_PAD, N_SLAB * LANES), f32)],
        compiler_params=pltpu.CompilerParams(dimension_semantics=("arbitrary", "arbitrary"),
                                             vmem_limit_bytes=VMEM_LIMIT),
        name="fourier",
    )(ab, ab, ab_m, ab_m, fm, twc, tws)


def _fourier_constants():
    n = N_SLAB * SLAB
    k2 = np.arange(1, SLAB)[:, None]
    t2 = np.arange(SLAB_PAD)[None, :]
    ang = 2.0 * np.pi * ((k2 * t2) % SLAB) / SLAB
    valid = (t2 < SLAB)
    fm = np.concatenate([np.cos(ang) * valid, np.sin(ang) * valid], axis=1)
    k1 = np.arange(N_SLAB)[:, None]
    ang2 = 2.0 * np.pi * ((k1 * t2) % n) / n
    lane_bcast = np.ones((1, 1, LANES))
    twc = (np.cos(ang2) * valid)[:, :, None] * lane_bcast
    tws = (np.sin(ang2) * valid)[:, :, None] * lane_bcast
    return (jnp.asarray(fm, f32), jnp.asarray(twc, f32), jnp.asarray(tws, f32))


def _mix_out_kernel(h1_ref, hs_ref, gate_ref, four_ref, gl_ref, fb_ref, gf_ref, wo_ref,
                    g2_ref, w1g_ref, w1u_ref, w2_ref, gfin_ref, o_ref,
                    wo_s, w1g_s, w1u_s, w2_s, mix_ref, act_ref):
    i = pl.program_id(0)

    @pl.when(i < N_FF)
    def _cast():
        _cast_ffn_chunk(i, w1g_ref, w1u_ref, w2_ref, w1g_s, w1u_s, w2_s)

    @pl.when(i < N_WO)
    def _cast_wo():
        wo_s[i] = wo_ref[...].astype(bf16)

    @pl.when(i >= N_FF)
    def _tile():
        y_l = _rms_norm(hs_ref[...] * gate_ref[...], gl_ref[...])
        y_f = _rms_norm(four_ref[...] + fb_ref[...], gf_ref[...])
        mix_ref[:, 0:LRU_WIDTH] = y_l.astype(bf16)
        mix_ref[:, LRU_WIDTH:] = y_f.astype(bf16)
        mixed = mix_ref[...]
        proj = jnp.concatenate([jnp.dot(mixed, wo_s[c], preferred_element_type=f32) for c in range(N_WO)], axis=1)
        h2 = h1_ref[...] + proj
        xn = _rms_norm(h2, g2_ref[...]).astype(bf16)
        h3 = h2 + 0.5 * _swiglu(xn, w1g_s, w1u_s, w2_s, act_ref)
        o_ref[...] = _rms_norm(h3, gfin_ref[...])


def _mix_out(h1, hs, gate, four, gl, fb, gf, w_out, g2, w_ffn_in, w_ffn_out, gfin, tile):
    rows = h1.shape[0]
    row_spec = lambda w: pl.BlockSpec((tile, w), lambda i: (jnp.maximum(i - N_FF, 0), 0))
    return pl.pallas_call(
        _mix_out_kernel,
        grid=(N_FF + rows // tile,),
        in_specs=[row_spec(D_MODEL), row_spec(LRU_WIDTH), row_spec(LRU_WIDTH), row_spec(FOURIER_WIDTH),
                  _resident((1, LRU_WIDTH)), _resident((1, FOURIER_WIDTH)), _resident((1, FOURIER_WIDTH)),
                  pl.BlockSpec((D_MODEL, MXU_COLS), lambda i: (0, jnp.minimum(i, N_WO - 1))),
                  _resident((1, D_MODEL)), *_ffn_weight_specs(), _resident((1, D_MODEL))],
        out_specs=row_spec(D_MODEL),
        out_shape=jax.ShapeDtypeStruct((rows, D_MODEL), f32),
        scratch_shapes=[pltpu.VMEM((N_WO, D_MODEL, MXU_COLS), bf16)] + _ffn_weight_scratch()
                       + [pltpu.VMEM((tile, D_MODEL), bf16), pltpu.VMEM((tile, D_FF), bf16)],
        compiler_params=pltpu.CompilerParams(dimension_semantics=("arbitrary",), vmem_limit_bytes=VMEM_LIMIT),
        name="mix_out",
    )(h1, hs, gate, four, gl, fb, gf, w_out, g2, w_ffn_in, w_ffn_in, w_ffn_out, gfin)


def _gate_weights(wa_f, wx_f, wa_b, wx_b, ba_f, bx_f, ba_b, bx_b):
    n_slab = LRU_WIDTH // LANES
    per = LANES // LRU_HEAD_DIM
    w = jnp.stack([wa_f, wx_f, wa_b, wx_b]).astype(f32).reshape(4, LRU_WIDTH, LRU_HEAD_DIM)
    rows_head = (np.arange(LRU_WIDTH) // LRU_HEAD_DIM) % per
    lanes_head = np.arange(LANES) // LRU_HEAD_DIM
    mask = jnp.asarray(0.5 * (rows_head[:, None] == lanes_head[None, :]), f32)
    wg = jnp.tile(w, (1, 1, per)) * mask
    wg = wg.reshape(4, n_slab, LANES, LANES).transpose(1, 2, 0, 3).reshape(n_slab, LANES, 4 * LANES)
    bg = jnp.stack([ba_f, bx_f, ba_b, bx_b]).astype(f32).reshape(4, n_slab, 1, LANES)
    bg = 0.5 * bg.transpose(1, 2, 0, 3).reshape(n_slab, 1, 4 * LANES)
    return wg.astype(bf16), bg


def kernel(x, meta_tokens, norm_ffn1, w_ffn1_in, w_ffn1_out, norm_mix, w_in, conv_w, conv_b, lru_wa_fwd, lru_ba_fwd, lru_wx_fwd, lru_bx_fwd, lru_lambda_fwd, lru_wa_bwd, lru_ba_bwd, lru_wx_bwd, lru_bx_bwd, lru_lambda_bwd, fourier_w, fourier_b, norm_lru_out, norm_fourier_out, w_out, norm_ffn2, w_ffn2_in, w_ffn2_out, norm_final):
    batch, seq, d = x.shape
    assert (d, seq + N_META) == (D_MODEL, N_SLAB * SLAB) and norm_ffn1.shape[0] == 1
    l = 0
    row = lambda v: v.astype(f32).reshape(1, -1)

    mix = _fourier_channel_mix(fourier_w[l])
    wg, bg = _gate_weights(lru_wa_fwd[l], lru_wx_fwd[l], lru_wa_bwd[l], lru_wx_bwd[l],
                           lru_ba_fwd[l], lru_bx_fwd[l], lru_ba_bwd[l], lru_bx_bwd[l])
    lam = jnp.stack([lru_lambda_fwd[l], lru_lambda_bwd[l]]).astype(f32)

    h1, gate, ab, ab_m, a_f, b_f, a_b, b_b = _ffn_in(
        x.astype(f32), meta_tokens.astype(f32), row(norm_ffn1[l]),
        w_ffn1_in[l].astype(f32), w_ffn1_out[l].astype(f32), row(norm_mix[l]), w_in[l].astype(f32), mix,
        conv_w[l].astype(f32), row(conv_b[l]), wg, bg, lam)

    hsum = _lru(a_f, b_f, a_b, b_b, seq)

    fm, twc, tws = _fourier_constants()
    four = _fourier(ab, ab_m, fm.astype(bf16), twc, tws, batch, seq)

    out = _mix_out(h1.reshape(batch * seq, d), hsum.reshape(batch * seq, LRU_WIDTH),
                   gate.reshape(batch * seq, LRU_WIDTH), four.reshape(batch * seq, FOURIER_WIDTH),
                   row(norm_lru_out[l]), row(fourier_b[l]), row(norm_fourier_out[l]), w_out[l].astype(f32),
                   row(norm_ffn2[l]), w_ffn2_in[l].astype(f32), w_ffn2_out[l].astype(f32), row(norm_final),
                   tile=ROW_TILE)
    return out.reshape(batch, seq, d).astype(x.dtype)
```

```python
import math

import numpy as np
import jax
import jax.numpy as jnp
from jax import lax
from jax.experimental import pallas as pl
from jax.experimental.pallas import tpu as pltpu

D_MODEL = 1024
N_META = 16
LRU_WIDTH = 512
LRU_HEAD_DIM = 64
FOURIER_WIDTH = 512
FOURIER_GROUPS = 8
FOURIER_GROUP_DIM = 64
D_FF = 2816
EPS = 1e-6
RG_LRU_C = 8.0

LANES = 128
SUBLANES = 8
ROW_TILE = 512
TILES_PER_BATCH = 16
STEPS_PER_BATCH = TILES_PER_BATCH + 2
GATE_PIECES = 8
SEQ_HEAD = 24
MXU_COLS = 256
N_FF = D_FF // MXU_COLS
N_WIN = (2 * LRU_WIDTH + FOURIER_WIDTH) // MXU_COLS
N_WO = D_MODEL // MXU_COLS
VMEM_LIMIT = 60 * 1024 * 1024

N_SLAB = N_META
SLAB = 513
SLAB_PAD = 520
N_SEG = 16
SEG_LEN = 513
SCAN_UNROLL = 9

f32 = jnp.float32
bf16 = jnp.bfloat16


def _rms_norm(x, g):
    return x * lax.rsqrt(jnp.mean(x * x, axis=-1, keepdims=True) + EPS) * g


def _sigmoid(x):
    return 0.5 * (1.0 + jnp.tanh(0.5 * x))


def _gelu_tanh(x):
    c = math.sqrt(2.0 / math.pi)
    return 0.5 * x * (1.0 + jnp.tanh(c * (x + 0.044715 * (x * x * x))))


def _swiglu(xn, w1g_s, w1u_s, w2_s, act_ref, between=None):
    for c in range(N_FF):
        g = jnp.dot(xn, w1g_s[c], preferred_element_type=f32)
        u = jnp.dot(xn, w1u_s[c], preferred_element_type=f32)
        act_ref[:, c * MXU_COLS:(c + 1) * MXU_COLS] = (g * _sigmoid(g) * u).astype(bf16)
        if between is not None:
            between(c)
    return jnp.dot(act_ref[...], w2_s[...], preferred_element_type=f32)


def _cast_ffn_chunk(i, w1g_ref, w1u_ref, w2_ref, w1g_s, w1u_s, w2_s):
    w1g_s[i] = w1g_ref[...].astype(bf16)
    w1u_s[i] = w1u_ref[...].astype(bf16)
    w2_s[pl.ds(pl.multiple_of(i * MXU_COLS, MXU_COLS), MXU_COLS), :] = w2_ref[...].astype(bf16)


def _ffn_weight_specs():
    last = N_FF - 1
    return [pl.BlockSpec((D_MODEL, MXU_COLS), lambda i: (0, jnp.minimum(i, last))),
            pl.BlockSpec((D_MODEL, MXU_COLS), lambda i: (0, N_FF + jnp.minimum(i, last))),
            pl.BlockSpec((MXU_COLS, D_MODEL), lambda i: (jnp.minimum(i, last), 0))]


def _ffn_weight_scratch():
    return [pltpu.VMEM((N_FF, D_MODEL, MXU_COLS), bf16), pltpu.VMEM((N_FF, D_MODEL, MXU_COLS), bf16),
            pltpu.VMEM((D_FF, D_MODEL), bf16)]


def _resident(shape):
    nd = len(shape)
    return pl.BlockSpec(shape, lambda *_: (0,) * nd, pipeline_mode=pl.Buffered(1))


def _chanmix_kernel(bd_ref, cd_ref, sd_ref, o_ref):
    hi = lax.Precision.HIGHEST
    for h in range(FOURIER_WIDTH // MXU_COLS):
        blk = slice(h * MXU_COLS, (h + 1) * MXU_COLS)
        bd = bd_ref[blk, blk]
        o_ref[h, :, 0:MXU_COLS] = jnp.dot(cd_ref[blk, blk], bd, preferred_element_type=f32,
                                          precision=hi).astype(o_ref.dtype)
        o_ref[h, :, MXU_COLS:] = jnp.dot(sd_ref[blk, blk], bd, preferred_element_type=f32,
                                         precision=hi).astype(o_ref.dtype)


def _fourier_channel_mix(fourier_w):
    g, n = FOURIER_GROUPS, FOURIER_GROUP_DIM
    eye = np.eye(g, dtype=np.float64)
    ang = 2.0 * np.pi * (np.outer(np.arange(n), np.arange(n)) % n) / n
    scale = 1.0 / math.sqrt((N_SLAB * SLAB) * n)
    cd = jnp.asarray(np.kron(eye, np.cos(ang)) * scale, f32)
    sd = jnp.asarray(np.kron(eye, -np.sin(ang)) * scale, f32)
    bd = jnp.tile(fourier_w.astype(f32).reshape(g * n, n), (1, g)) * jnp.asarray(np.kron(eye, np.ones((n, n))), f32)
    return pl.pallas_call(
        _chanmix_kernel,
        out_shape=jax.ShapeDtypeStruct((FOURIER_WIDTH // MXU_COLS, MXU_COLS, 2 * MXU_COLS), bf16),
        name="fourier_channel_mix",
    )(bd, cd, sd)


def _lru_gate_rows(seq_s, r0, n, cw, cb, wg_ref, bg_ref, k_f, k_b, outs, slabs=range(LRU_WIDTH // LANES)):
    af_ref, bf_ref, ab_ref, bb_ref = outs

    def gates(half_pre_r, half_pre_i, x, k_dir):
        t = jnp.tanh(k_dir * jnp.tanh(half_pre_r) + k_dir)
        q = 1.0 / (1.0 - t)
        a = (1.0 + t) * q
        y = -t
        root = jnp.where(y > 0.0, y * lax.rsqrt(y), 0.0)
        return a, (root * q) * ((1.0 + jnp.tanh(half_pre_i)) * x)

    for s in slabs:
        lanes = slice(s * LANES, (s + 1) * LANES)
        x = (cw[0:1, lanes] * seq_s[r0 + 6:r0 + 6 + n, lanes] + cw[1:2, lanes] * seq_s[r0 + 7:r0 + 7 + n, lanes]
             + cw[2:3, lanes] * seq_s[r0 + 8:r0 + 8 + n, lanes] + cw[3:4, lanes] * seq_s[r0 + 9:r0 + 9 + n, lanes]
             + cb[:, lanes])
        pre = jnp.dot(x.astype(bf16), wg_ref[s], preferred_element_type=f32) + bg_ref[s]
        a_f, b_f = gates(pre[:, 0:LANES], pre[:, LANES:2 * LANES], x, k_f[:, lanes])
        a_b, b_b = gates(pre[:, 2 * LANES:3 * LANES], pre[:, 3 * LANES:4 * LANES], x, k_b[:, lanes])
        af_ref[s, r0:r0 + n, :] = a_f
        bf_ref[s, r0:r0 + n, :] = b_f.astype(bf_ref.dtype)
        ab_ref[s, r0:r0 + n, :] = a_b
        bb_ref[s, r0:r0 + n, :] = b_b.astype(bb_ref.dtype)


def _ffn_in_kernel(x_ref, xm_ref, g1_ref, w1g_ref, w1u_ref, w2_ref, g2_ref, win_ref, mix_ref,
                   cw_ref, cb_ref, wg_ref, bg_ref, lam_ref,
                   h1_ref, gate_ref, ab_ref, abm_ref, af_ref, bf_ref, abk_ref, bb_ref,
                   w1g_s, w1u_s, w2_s, win_s, act_ref, actm_ref, seq_s, carry_s, lrum_s):
    i = pl.program_id(0)
    q = jnp.maximum(i - (N_FF + 1), 0)
    j = q % STEPS_PER_BATCH
    in_batch = i > N_FF
    gate_outs = (af_ref, bf_ref, abk_ref, bb_ref)

    @pl.when(i < N_FF)
    def _cast():
        _cast_ffn_chunk(i, w1g_ref, w1u_ref, w2_ref, w1g_s, w1u_s, w2_s)

    @pl.when(i < N_WIN)
    def _cast_win():
        win_s[i] = win_ref[...].astype(bf16)

    @pl.when(i == 0)
    def _init():
        seq_s[...] = jnp.zeros(seq_s.shape, f32)
        carry_s[...] = jnp.zeros(carry_s.shape, f32)

    def rows_body(x, act, between=None):
        xn = _rms_norm(x, g1_ref[...]).astype(bf16)
        h1 = x + 0.5 * _swiglu(xn, w1g_s, w1u_s, w2_s, act, between)
        xn2 = _rms_norm(h1, g2_ref[...]).astype(bf16)
        u = [jnp.dot(xn2, win_s[c], preferred_element_type=f32) for c in range(N_WIN)]
        n_lru = LRU_WIDTH // MXU_COLS
        lru = jnp.concatenate(u[0:n_lru], axis=1)
        gate = jnp.concatenate(u[n_lru:2 * n_lru], axis=1)
        ab = [jnp.dot(v.astype(bf16), mix_ref[h], preferred_element_type=f32)
              for h, v in enumerate(u[2 * n_lru:])]
        a_b = jnp.concatenate([p[:, 0:MXU_COLS] for p in ab] + [p[:, MXU_COLS:] for p in ab], axis=1)
        return h1, lru, gate, a_b

    def gate_params():
        nlam = -lam_ref[...]
        sp = jnp.maximum(nlam, 0.0) + jnp.log1p(jnp.exp(-jnp.abs(nlam)))
        return cw_ref[...], cb_ref[...], (-0.25 * RG_LRU_C) * sp[0:1, :], (-0.25 * RG_LRU_C) * sp[1:2, :]

    @pl.when(i == N_FF)
    def _meta():
        _, lru, _, a_b = rows_body(xm_ref[...], actm_ref)
        lrum_s[...] = lru
        abm_ref[...] = a_b

    @pl.when(in_batch & (j < TILES_PER_BATCH))
    def _tile():
        cw, cb, k_f, k_b = gate_params()
        n_slab = LRU_WIDTH // LANES
        piece = ROW_TILE * n_slab // GATE_PIECES

        def between(c):
            if c < GATE_PIECES:
                slab, part = c % n_slab, c // n_slab
                _lru_gate_rows(seq_s, part * piece, piece, cw, cb, wg_ref, bg_ref, k_f, k_b, gate_outs,
                               slabs=(slab,))

        h1, lru, gate, a_b = rows_body(x_ref[...], act_ref, between)
        h1_ref[...] = h1
        gate_ref[...] = _gelu_tanh(gate)
        ab_ref[...] = a_b
        head = jnp.concatenate([jnp.zeros((SUBLANES, LRU_WIDTH), f32), lrum_s[...]], axis=0)
        seq_s[0:SEQ_HEAD, :] = jnp.where(j == 0, head, carry_s[...])
        seq_s[SEQ_HEAD:SEQ_HEAD + ROW_TILE, :] = lru
        carry_s[...] = lru[ROW_TILE - SEQ_HEAD:ROW_TILE, :]

    @pl.when(in_batch & (j == TILES_PER_BATCH))
    def _last_block():
        cw, cb, k_f, k_b = gate_params()
        _lru_gate_rows(seq_s, 0, ROW_TILE, cw, cb, wg_ref, bg_ref, k_f, k_b, gate_outs)
        seq_s[0:SEQ_HEAD, :] = carry_s[...]
        seq_s[SEQ_HEAD:SEQ_HEAD + SUBLANES, :] = jnp.zeros((SUBLANES, LRU_WIDTH), f32)

    @pl.when(in_batch & (j == TILES_PER_BATCH + 1))
    def _tail_block():
        cw, cb, k_f, k_b = gate_params()
        _lru_gate_rows(seq_s, 0, N_META, cw, cb, wg_ref, bg_ref, k_f, k_b, gate_outs)


def _ffn_in(x3d, xm, g1, w_ffn_in, w_ffn_out, g2, w_in, mix, conv_w, conv_b, wg, bg, lam):
    batch, seq, _ = x3d.shape
    assert seq == TILES_PER_BATCH * ROW_TILE
    n_meta = xm.shape[0]
    lead = N_FF + 1

    def tile_index(i):
        q = jnp.maximum(i - lead, 0)
        return q // STEPS_PER_BATCH, jnp.minimum(q % STEPS_PER_BATCH, TILES_PER_BATCH - 1)

    def block_index(i):
        q = jnp.maximum(i - lead, 0)
        return q // STEPS_PER_BATCH, jnp.clip(q % STEPS_PER_BATCH - 1, 0, TILES_PER_BATCH)

    row_spec = lambda w: pl.BlockSpec((None, ROW_TILE, w), lambda i: (*tile_index(i), 0))
    n_slab = LRU_WIDTH // LANES

    def seq_index(i):
        b, blk = block_index(i)
        return b, 0, blk, 0

    seq_spec = pl.BlockSpec((None, n_slab, ROW_TILE, LANES), seq_index)
    whole = lambda shape: pl.BlockSpec(shape, lambda i: (0,) * len(shape))
    coeff = lambda dt: jax.ShapeDtypeStruct((batch, n_slab, n_meta + seq, LANES), dt)
    return pl.pallas_call(
        _ffn_in_kernel,
        grid=(lead + batch * STEPS_PER_BATCH,),
        in_specs=[row_spec(D_MODEL), _resident(xm.shape), _resident((1, D_MODEL)), *_ffn_weight_specs(),
                  _resident((1, D_MODEL)),
                  pl.BlockSpec((D_MODEL, MXU_COLS), lambda i: (0, jnp.minimum(i, N_WIN - 1))),
                  _resident(mix.shape), _resident(conv_w.shape), _resident(conv_b.shape),
                  _resident(wg.shape), _resident(bg.shape), _resident(lam.shape)],
        out_specs=[row_spec(D_MODEL), row_spec(LRU_WIDTH), row_spec(2 * FOURIER_WIDTH),
                   whole((n_meta, 2 * FOURIER_WIDTH)), seq_spec, seq_spec, seq_spec, seq_spec],
        out_shape=[jax.ShapeDtypeStruct((batch, seq, D_MODEL), f32),
                   jax.ShapeDtypeStruct((batch, seq, LRU_WIDTH), f32),
                   jax.ShapeDtypeStruct((batch, seq, 2 * FOURIER_WIDTH), f32),
                   jax.ShapeDtypeStruct((n_meta, 2 * FOURIER_WIDTH), f32),
                   coeff(f32), coeff(bf16), coeff(f32), coeff(bf16)],
        scratch_shapes=_ffn_weight_scratch()
                       + [pltpu.VMEM((N_WIN, D_MODEL, MXU_COLS), bf16),
                          pltpu.VMEM((ROW_TILE, D_FF), bf16), pltpu.VMEM((n_meta, D_FF), bf16),
                          pltpu.VMEM((SEQ_HEAD + ROW_TILE, LRU_WIDTH), f32),
                          pltpu.VMEM((SEQ_HEAD, LRU_WIDTH), f32), pltpu.VMEM((n_meta, LRU_WIDTH), f32)],
        compiler_params=pltpu.CompilerParams(dimension_semantics=("arbitrary",), vmem_limit_bytes=VMEM_LIMIT),
        name="ffn_in",
    )(x3d, xm, g1, w_ffn_in, w_ffn_in, w_ffn_out, g2, w_in, mix, conv_w, conv_b, wg, bg, lam)


def _lru_kernel(af_ref, bfh_ref, ab_ref, bbh_ref, o_ref, bf_ref, bb_ref, hf_ref, hb_ref, ends_ref, carry_ref):
    t_all = N_SEG * SEG_LEN
    bf_ref[...] = bfh_ref[...].astype(f32)
    bb_ref[...] = bbh_ref[...].astype(f32)
    zero8 = jnp.zeros((SUBLANES, LANES), f32)
    half = SUBLANES * SEG_LEN

    def seg_rows(j, hi):
        return pl.ds(j + hi * half, SUBLANES, stride=SEG_LEN)

    def pass1(j, st):
        hf0, hf1, pf0, pf1, hb0, hb1, pb0, pb1 = st
        jb = SEG_LEN - 1 - j
        a0 = af_ref[seg_rows(j, 0), :]
        a1 = af_ref[seg_rows(j, 1), :]
        hf0 = a0 * hf0 + bf_ref[seg_rows(j, 0), :]
        hf1 = a1 * hf1 + bf_ref[seg_rows(j, 1), :]
        pf0 = a0 * pf0
        pf1 = a1 * pf1
        c0 = ab_ref[seg_rows(jb, 0), :]
        c1 = ab_ref[seg_rows(jb, 1), :]
        hb0 = c0 * hb0 + bb_ref[seg_rows(jb, 0), :]
        hb1 = c1 * hb1 + bb_ref[seg_rows(jb, 1), :]
        pb0 = c0 * pb0
        pb1 = c1 * pb1
        return hf0, hf1, pf0, pf1, hb0, hb1, pb0, pb1

    one8 = jnp.ones((SUBLANES, LANES), f32)
    st = lax.fori_loop(0, SEG_LEN, pass1, (zero8, zero8, one8, one8, zero8, zero8, one8, one8),
                       unroll=SCAN_UNROLL)
    for idx, v in enumerate(st):
        ends_ref[idx * SUBLANES:(idx + 1) * SUBLANES, :] = v

    c = jnp.zeros((1, LANES), f32)
    for s in range(N_SEG):
        carry_ref[s:s + 1, :] = c
        c = ends_ref[s:s + 1, :] + ends_ref[N_SEG + s:N_SEG + s + 1, :] * c
    c = jnp.zeros((1, LANES), f32)
    for s in range(N_SEG - 1, -1, -1):
        carry_ref[N_SEG + s:N_SEG + s + 1, :] = c
        c = ends_ref[2 * N_SEG + s:2 * N_SEG + s + 1, :] + ends_ref[3 * N_SEG + s:3 * N_SEG + s + 1, :] * c

    def pass2(j, st):
        hf0, hf1, hb0, hb1 = st
        jb = SEG_LEN - 1 - j
        hf0 = af_ref[seg_rows(j, 0), :] * hf0 + bf_ref[seg_rows(j, 0), :]
        hf1 = af_ref[seg_rows(j, 1), :] * hf1 + bf_ref[seg_rows(j, 1), :]
        hf_ref[seg_rows(j, 0), :] = hf0
        hf_ref[seg_rows(j, 1), :] = hf1
        hb0 = ab_ref[seg_rows(jb, 0), :] * hb0 + bb_ref[seg_rows(jb, 0), :]
        hb1 = ab_ref[seg_rows(jb, 1), :] * hb1 + bb_ref[seg_rows(jb, 1), :]
        hb_ref[seg_rows(jb, 0), :] = hb0
        hb_ref[seg_rows(jb, 1), :] = hb1
        return hf0, hf1, hb0, hb1

    lax.fori_loop(0, SEG_LEN, pass2,
                  (carry_ref[0:SUBLANES, :], carry_ref[SUBLANES:2 * SUBLANES, :],
                   carry_ref[2 * SUBLANES:3 * SUBLANES, :], carry_ref[3 * SUBLANES:4 * SUBLANES, :]),
                  unroll=SCAN_UNROLL)

    o_ref[...] = hf_ref[N_META:t_all, :] + hb_ref[N_META:t_all, :]


def _lru(a_f, b_f, a_b, b_b, seq):
    batch, n_slab, t_all, _ = a_f.shape
    slab = pl.BlockSpec((None, None, t_all, LANES), lambda b, s: (b, s, 0, 0))
    return pl.pallas_call(
        _lru_kernel,
        grid=(batch, n_slab),
        in_specs=[slab] * 4,
        out_specs=pl.BlockSpec((None, seq, LANES), lambda b, s: (b, 0, s)),
        out_shape=jax.ShapeDtypeStruct((batch, seq, LRU_WIDTH), f32),
        scratch_shapes=[pltpu.VMEM((t_all, LANES), f32)] * 4
                       + [pltpu.VMEM((4 * N_SEG, LANES), f32), pltpu.VMEM((2 * N_SEG, LANES), f32)],
        compiler_params=pltpu.CompilerParams(dimension_semantics=("arbitrary", "arbitrary"),
                                             vmem_limit_bytes=VMEM_LIMIT),
        name="lru",
    )(a_f, b_f, a_b, b_b)


def _cmul_const(x, c, s):
    re, im = x
    tol = 1e-12
    if abs(s) < tol:
        return (re, im) if c > 0 else (-re, -im)
    if abs(c) < tol:
        return (-im, re) if s > 0 else (im, -re)
    if abs(abs(c) - abs(s)) < tol:
        m = abs(c)
        sc, ss = (1.0 if c > 0 else -1.0), (1.0 if s > 0 else -1.0)
        return (m * (sc * re - ss * im), m * (ss * re + sc * im))
    return (c * re - s * im, s * re + c * im)


def _fft(xs):
    n = len(xs)
    if n == 1:
        return xs
    ev = _fft(xs[0::2])
    od = _fft(xs[1::2])
    out = [None] * n
    for k in range(n // 2):
        ang = -2.0 * math.pi * k / n
        t = _cmul_const(od[k], math.cos(ang), math.sin(ang))
        out[k] = (ev[k][0] + t[0], ev[k][1] + t[1])
        out[k + n // 2] = (ev[k][0] - t[0], ev[k][1] - t[1])
    return out


def _fourier_kernel(xa_ref, xb_ref, ma_ref, mb_ref, fm_ref, twc_ref, tws_ref, o_ref,
                    ha_ref, hb_ref, ta_ref, tb_ref, y_ref):
    tail0 = (N_SLAB - 1) * SLAB - N_META
    for m_ref, x_ref, h_ref, t_ref in ((ma_ref, xa_ref, ha_ref, ta_ref), (mb_ref, xb_ref, hb_ref, tb_ref)):
        h_ref[0:N_META, :] = m_ref[...]
        h_ref[N_META:SLAB_PAD, :] = x_ref[0:SLAB_PAD - N_META, :]
        t_ref[SLAB - 1:SLAB_PAD, :] = jnp.zeros((SLAB_PAD - SLAB + 1, LANES), f32)
        t_ref[0:SLAB - 1, :] = x_ref[tail0:tail0 + SLAB - 1, :]
        t_ref[SLAB - 1:SLAB, :] = x_ref[tail0 + SLAB - 1:tail0 + SLAB, :]

    def stage1(it, carry):
        r = pl.multiple_of(it * SUBLANES, SUBLANES)
        rows = pl.ds(r, SUBLANES)

        def slab_rows(x_ref, h_ref, t_ref, t1):
            if t1 == 0:
                return h_ref[rows, :]
            if t1 == N_SLAB - 1:
                return t_ref[rows, :]
            return x_ref[pl.ds(r + (t1 * SLAB - N_META), SUBLANES), :]

        zs = [(slab_rows(xa_ref, ha_ref, ta_ref, t1), slab_rows(xb_ref, hb_ref, tb_ref, t1))
              for t1 in range(N_SLAB)]
        ys = _fft(zs)
        for k1 in range(N_SLAB):
            yr, yi = ys[k1]
            if k1 > 0:
                cw = twc_ref[k1, rows, :]
                sw = tws_ref[k1, rows, :]
                yr, yi = yr * cw + yi * sw, yi * cw - yr * sw
            y_ref[rows, k1 * LANES:(k1 + 1) * LANES] = yr
            y_ref[pl.ds(r + SLAB_PAD, SUBLANES), k1 * LANES:(k1 + 1) * LANES] = yi
        return carry

    lax.fori_loop(0, SLAB_PAD // SUBLANES, stage1, 0)

    for p in range(N_SLAB // 2):
        yp = y_ref[:, 2 * p * LANES:(2 * p + 2) * LANES].astype(bf16)
        g = jnp.dot(fm_ref[...], yp, preferred_element_type=f32)
        o_ref[pl.ds(2 * p, SLAB - 1, stride=N_SLAB), :] = g[:, 0:LANES]
        o_ref[pl.ds(2 * p + 1, SLAB - 1, stride=N_SLAB), :] = g[:, LANES:2 * LANES]


def _fourier(ab, ab_m, fm, twc, tws, batch, seq):
    n_blk = FOURIER_WIDTH // LANES
    x_spec = lambda off: pl.BlockSpec((None, seq, LANES), lambda b, j: (b, 0, j + off))
    m_spec = lambda off: pl.BlockSpec((N_META, LANES), lambda b, j: (0, j + off))
    return pl.pallas_call(
        _fourier_kernel,
        grid=(batch, n_blk),
        in_specs=[x_spec(0), x_spec(n_blk), m_spec(0), m_spec(n_blk),
                  _resident(fm.shape), _resident(twc.shape), _resident(tws.shape)],
        out_specs=pl.BlockSpec((None, seq, LANES), lambda b, j: (b, 0, j)),
        out_shape=jax.ShapeDtypeStruct((batch, seq, FOURIER_WIDTH), f32),
        scratch_shapes=[pltpu.VMEM((SLAB_PAD, LANES), f32)] * 4
                       + [pltpu.VMEM((2 * SLAB_PAD, N_SLAB * LANES), f32)],
        compiler_params=pltpu.CompilerParams(dimension_semantics=("arbitrary", "arbitrary"),
                                             vmem_limit_bytes=VMEM_LIMIT),
        name="fourier",
    )(ab, ab, ab_m, ab_m, fm, twc, tws)


def _fourier_constants():
    n = N_SLAB * SLAB
    k2 = np.arange(1, SLAB)[:, None]
    t2 = np.arange(SLAB_PAD)[None, :]
    ang = 2.0 * np.pi * ((k2 * t2) % SLAB) / SLAB
    valid = (t2 < SLAB)
    fm = np.concatenate([np.cos(ang) * valid, np.sin(ang) * valid], axis=1)
    k1 = np.arange(N_SLAB)[:, None]
    ang2 = 2.0 * np.pi * ((k1 * t2) % n) / n
    lane_bcast = np.ones((1, 1, LANES))
    twc = (np.cos(ang2) * valid)[:, :, None] * lane_bcast
    tws = (np.sin(ang2) * valid)[:, :, None] * lane_bcast
    return (jnp.asarray(fm, f32), jnp.asarray(twc, f32), jnp.asarray(tws, f32))


def _mix_out_kernel(h1_ref, hs_ref, gate_ref, four_ref, gl_ref, fb_ref, gf_ref, wo_ref,
                    g2_ref, w1g_ref, w1u_ref, w2_ref, gfin_ref, o_ref,
                    wo_s, w1g_s, w1u_s, w2_s, mix_ref, act_ref):
    i = pl.program_id(0)

    @pl.when(i < N_FF)
    def _cast():
        _cast_ffn_chunk(i, w1g_ref, w1u_ref, w2_ref, w1g_s, w1u_s, w2_s)

    @pl.when(i < N_WO)
    def _cast_wo():
        wo_s[i] = wo_ref[...].astype(bf16)

    @pl.when(i >= N_FF)
    def _tile():
        y_l = _rms_norm(hs_ref[...] * gate_ref[...], gl_ref[...])
        y_f = _rms_norm(four_ref[...] + fb_ref[...], gf_ref[...])
        mix_ref[:, 0:LRU_WIDTH] = y_l.astype(bf16)
        mix_ref[:, LRU_WIDTH:] = y_f.astype(bf16)
        mixed = mix_ref[...]
        proj = jnp.concatenate([jnp.dot(mixed, wo_s[c], preferred_element_type=f32) for c in range(N_WO)], axis=1)
        h2 = h1_ref[...] + proj
        xn = _rms_norm(h2, g2_ref[...]).astype(bf16)
        h3 = h2 + 0.5 * _swiglu(xn, w1g_s, w1u_s, w2_s, act_ref)
        o_ref[...] = _rms_norm(h3, gfin_ref[...])


def _mix_out(h1, hs, gate, four, gl, fb, gf, w_out, g2, w_ffn_in, w_ffn_out, gfin, tile):
    rows = h1.shape[0]
    row_spec = lambda w: pl.BlockSpec((tile, w), lambda i: (jnp.maximum(i - N_FF, 0), 0))
    return pl.pallas_call(
        _mix_out_kernel,
        grid=(N_FF + rows // tile,),
        in_specs=[row_spec(D_MODEL), row_spec(LRU_WIDTH), row_spec(LRU_WIDTH), row_spec(FOURIER_WIDTH),
                  _resident((1, LRU_WIDTH)), _resident((1, FOURIER_WIDTH)), _resident((1, FOURIER_WIDTH)),
                  pl.BlockSpec((D_MODEL, MXU_COLS), lambda i: (0, jnp.minimum(i, N_WO - 1))),
                  _resident((1, D_MODEL)), *_ffn_weight_specs(), _resident((1, D_MODEL))],
        out_specs=row_spec(D_MODEL),
        out_shape=jax.ShapeDtypeStruct((rows, D_MODEL), f32),
        scratch_shapes=[pltpu.VMEM((N_WO, D_MODEL, MXU_COLS), bf16)] + _ffn_weight_scratch()
                       + [pltpu.VMEM((tile, D_MODEL), bf16), pltpu.VMEM((tile, D_FF), bf16)],
        compiler_params=pltpu.CompilerParams(dimension_semantics=("arbitrary",), vmem_limit_bytes=VMEM_LIMIT),
        name="mix_out",
    )(h1, hs, gate, four, gl, fb, gf, w_out, g2, w_ffn_in, w_ffn_in, w_ffn_out, gfin)


def _gate_weights(wa_f, wx_f, wa_b, wx_b, ba_f, bx_f, ba_b, bx_b):
    n_slab = LRU_WIDTH // LANES
    per = LANES // LRU_HEAD_DIM
    w = jnp.stack([wa_f, wx_f, wa_b, wx_b]).astype(f32).reshape(4, LRU_WIDTH, LRU_HEAD_DIM)
    rows_head = (np.arange(LRU_WIDTH) // LRU_HEAD_DIM) % per
    lanes_head = np.arange(LANES) // LRU_HEAD_DIM
    mask = jnp.asarray(0.5 * (rows_head[:, None] == lanes_head[None, :]), f32)
    wg = jnp.tile(w, (1, 1, per)) * mask
    wg = wg.reshape(4, n_slab, LANES, LANES).transpose(1, 2, 0, 3).reshape(n_slab, LANES, 4 * LANES)
    bg = jnp.stack([ba_f, bx_f, ba_b, bx_b]).astype(f32).reshape(4, n_slab, 1, LANES)
    bg = 0.5 * bg.transpose(1, 2, 0, 3).reshape(n_slab, 1, 4 * LANES)
    return wg.astype(bf16), bg


def kernel(x, meta_tokens, norm_ffn1, w_ffn1_in, w_ffn1_out, norm_mix, w_in, conv_w, conv_b, lru_wa_fwd, lru_ba_fwd, lru_wx_fwd, lru_bx_fwd, lru_lambda_fwd, lru_wa_bwd, lru_ba_bwd, lru_wx_bwd, lru_bx_bwd, lru_lambda_bwd, fourier_w, fourier_b, norm_lru_out, norm_fourier_out, w_out, norm_ffn2, w_ffn2_in, w_ffn2_out, norm_final):
    batch, seq, d = x.shape
    assert (d, seq + N_META) == (D_MODEL, N_SLAB * SLAB) and norm_ffn1.shape[0] == 1
    l = 0
    row = lambda v: v.astype(f32).reshape(1, -1)

    mix = _fourier_channel_mix(fourier_w[l])
    wg, bg = _gate_weights(lru_wa_fwd[l], lru_wx_fwd[l], lru_wa_bwd[l], lru_wx_bwd[l],
                           lru_ba_fwd[l], lru_bx_fwd[l], lru_ba_bwd[l], lru_bx_bwd[l])
    lam = jnp.stack([lru_lambda_fwd[l], lru_lambda_bwd[l]]).astype(f32)

    h1, gate, ab, ab_m, a_f, b_f, a_b, b_b = _ffn_in(
        x.astype(f32), meta_tokens.astype(f32), row(norm_ffn1[l]),
        w_ffn1_in[l].astype(f32), w_ffn1_out[l].astype(f32), row(norm_mix[l]), w_in[l].astype(f32), mix,
        conv_w[l].astype(f32), row(conv_b[l]), wg, bg, lam)

    hsum = _lru(a_f, b_f, a_b, b_b, seq)

    fm, twc, tws = _fourier_constants()
    four = _fourier(ab, ab_m, fm.astype(bf16), twc, tws, batch, seq)

    out = _mix_out(h1.reshape(batch * seq, d), hsum.reshape(batch * seq, LRU_WIDTH),
                   gate.reshape(batch * seq, LRU_WIDTH), four.reshape(batch * seq, FOURIER_WIDTH),
                   row(norm_lru_out[l]), row(fourier_b[l]), row(norm_fourier_out[l]), w_out[l].astype(f32),
                   row(norm_ffn2[l]), w_ffn2_in[l].astype(f32), w_ffn2_out[l].astype(f32), row(norm_final),
                   tile=ROW_TILE)
    return out.reshape(batch, seq, d).astype(x.dtype)
```
